```python
import math
import jax, jax.numpy as jnp
from jax import lax
import numpy as np

D_MODEL = 1024
BATCH = 8
SEQ = 8192
DEPTH = 1
DEC_BATCH = 2
DEC_SEQ = 16384
PAST_LEN = 128

CONV_WIDTH = D_MODEL // 2
CONV_K = 3
ATTN_HEADS = 4
ATTN_HEAD_DIM = 64
ATTN_V_DIM = 2 * ATTN_HEAD_DIM
ATTN_WIDTH = ATTN_HEADS * ATTN_V_DIM
QK_WIDTH = ATTN_HEADS * 2 * ATTN_HEAD_DIM
MIX_WIDTH = CONV_WIDTH + ATTN_WIDTH
IN_COLS = 3 * CONV_WIDTH + 2 * QK_WIDTH + ATTN_WIDTH
ROPE_DIM = ATTN_HEAD_DIM // 4
ROPE_THETA = 500000.0
Q_BLOCK = 128
MEM_TOKENS = 256
MEM_HEADS = 4
MEM_HEAD_DIM = 128
MEM_WIDTH = MEM_HEADS * MEM_HEAD_DIM
D_FF = 4 * D_MODEL
EPS = 1e-6

kernel_name = 'hybrid_conv_diffattn_encoder'


def rmsnorm(x, g):
    xf = x.astype(jnp.float32)
    y = xf * lax.rsqrt(jnp.mean(xf * xf, axis=-1, keepdims=True) + EPS) * g.astype(jnp.float32)
    return y.astype(x.dtype)


def rope_partial(t, cos, sin):
    half = ROPE_DIM // 2
    c = cos[None, :, None, None, :].astype(t.dtype)
    s = sin[None, :, None, None, :].astype(t.dtype)
    r1 = t[..., :half]
    r2 = t[..., half:ROPE_DIM]
    return jnp.concatenate([r1 * c - r2 * s, r2 * c + r1 * s, t[..., ROPE_DIM:]], axis=-1)


def short_conv_mixer(u_b, u_c, u_v, w, b):
    S = u_v.shape[1]
    z = u_c * u_v
    zp = jnp.pad(z, ((0, 0), (1, 1), (0, 0)))
    conv = zp[:, :S] * w[0] + zp[:, 1:S + 1] * w[1] + zp[:, 2:] * w[2] + b
    return u_b * conv


def diff_attention(q, k, v, lam):
    B, S, H, _, dh = q.shape
    nb = S // Q_BLOCK
    scale = dh ** -0.5
    qb = q.reshape(B, nb, Q_BLOCK, H, 2, dh).transpose(1, 0, 2, 3, 4, 5)

    def block(qi):
        s = jnp.einsum('bqhcd,bkhcd->bhcqk', qi, k).astype(jnp.float32) * scale
        p = jax.nn.softmax(s, axis=-1)
        a = p[:, :, 0] - lam * p[:, :, 1]
        return jnp.einsum('bhqk,bkhe->bqhe', a.astype(v.dtype), v)

    o = lax.map(block, qb)
    return o.transpose(1, 0, 2, 3, 4).reshape(B, S, H, v.shape[-1])


def memory_attention(h, m, w_q, w_kv, w_o, gq, gk):
    B, S, _ = h.shape
    M = m.shape[1]
    q = (h @ w_q).reshape(B, S, MEM_HEADS, MEM_HEAD_DIM)
    kv = (m @ w_kv).reshape(B, M, 2, MEM_HEADS, MEM_HEAD_DIM)
    k = rmsnorm(kv[:, :, 0], gk)
    v = kv[:, :, 1]
    q = rmsnorm(q, gq)
    s = jnp.einsum('bqhd,bkhd->bhqk', q, k).astype(jnp.float32) * (MEM_HEAD_DIM ** -0.5)
    p = jax.nn.softmax(s, axis=-1)
    o = jnp.einsum('bhqk,bkhd->bqhd', p.astype(v.dtype), v).reshape(B, S, MEM_WIDTH)
    return o @ w_o


def encoder_trunk(x, mem, g_mix, w_in, conv_w, conv_b, q_norm, k_norm, lambda_q1, lambda_k1,
                  lambda_q2, lambda_k2, g_subln, w_out, g_memq, g_memkv, wm_q, wm_kv,
                  q_norm_mem, k_norm_mem, wm_o, g_mlp, w_ff1, w_ff2):
    B, S, _ = x.shape
    pos = jnp.arange(S, dtype=jnp.float32)
    inv_freq = ROPE_THETA ** (-jnp.arange(0, ROPE_DIM, 2, dtype=jnp.float32) / ROPE_DIM)
    ang = pos[:, None] * inv_freq[None, :]
    cos, sin = jnp.cos(ang), jnp.sin(ang)
    splits = [CONV_WIDTH, 2 * CONV_WIDTH, 3 * CONV_WIDTH,
              3 * CONV_WIDTH + QK_WIDTH, 3 * CONV_WIDTH + 2 * QK_WIDTH]
    for l in range(DEPTH):
        h = rmsnorm(x, g_mix[l])
        u = h @ w_in[l]
        u_b, u_c, u_v, q, k, v = jnp.split(u, splits, axis=-1)
        y_conv = short_conv_mixer(u_b, u_c, u_v, conv_w[l], conv_b[l])

        q = q.reshape(B, S, ATTN_HEADS, 2, ATTN_HEAD_DIM)
        k = k.reshape(B, S, ATTN_HEADS, 2, ATTN_HEAD_DIM)
        v = v.reshape(B, S, ATTN_HEADS, ATTN_V_DIM)
        q = rope_partial(rmsnorm(q, q_norm[l]), cos, sin)
        k = rope_partial(rmsnorm(k, k_norm[l]), cos, sin)
        lam_init = 0.8 - 0.6 * math.exp(-0.3 * l)
        lam = (jnp.exp(jnp.sum(lambda_q1[l].astype(jnp.float32) * lambda_k1[l].astype(jnp.float32)))
               - jnp.exp(jnp.sum(lambda_q2[l].astype(jnp.float32) * lambda_k2[l].astype(jnp.float32)))
               + lam_init)
        o = diff_attention(q, k, v, lam)
        o = rmsnorm(o, g_subln[l]) * (1.0 - lam_init)
        mix = jnp.concatenate([y_conv, o.reshape(B, S, ATTN_WIDTH)], axis=-1)
        x = x + mix @ w_out[l]

        x = x + memory_attention(rmsnorm(x, g_memq[l]), rmsnorm(mem, g_memkv[l]),
                                 wm_q[l], wm_kv[l], wm_o[l], q_norm_mem[l], k_norm_mem[l])

        hf = rmsnorm(x, g_mlp[l]) @ w_ff1[l]
        x = x + jnp.square(jax.nn.relu(hf)) @ w_ff2[l]
    return x


def setup_inputs(seed: int = 0) -> dict:
    key = jax.random.key(seed)
    ks = jax.random.split(key, 32)
    f32 = jnp.float32

    def nrm(k, shape, scale):
        return jax.random.normal(k, shape, f32) * scale

    def gain(k, n):
        return 1.0 + 0.02 * jax.random.normal(k, (DEPTH, n), f32)

    return {
        'x_prompt': nrm(ks[0], (BATCH, SEQ, D_MODEL), 1.0),
        'x_sample': nrm(ks[1], (DEC_BATCH, DEC_SEQ, D_MODEL), 1.0),
        'mem_prompt': nrm(ks[2], (BATCH, MEM_TOKENS, D_MODEL), 1.0),
        'mem_sample': nrm(ks[3], (DEC_BATCH, MEM_TOKENS, D_MODEL), 1.0),
        'g_mix': gain(ks[4], D_MODEL),
        'w_in': nrm(ks[5], (DEPTH, D_MODEL, IN_COLS), D_MODEL ** -0.5),
        'conv_w': nrm(ks[6], (DEPTH, CONV_K, CONV_WIDTH), CONV_K ** -0.5),
        'conv_b': nrm(ks[7], (DEPTH, CONV_WIDTH), 0.02),
        'q_norm': gain(ks[8], ATTN_HEAD_DIM),
        'k_norm': gain(ks[9], ATTN_HEAD_DIM),
        'lambda_q1': nrm(ks[10], (DEPTH, ATTN_HEAD_DIM), 0.1),
        'lambda_k1': nrm(ks[11], (DEPTH, ATTN_HEAD_DIM), 0.1),
        'lambda_q2': nrm(ks[12], (DEPTH, ATTN_HEAD_DIM), 0.1),
        'lambda_k2': nrm(ks[13], (DEPTH, ATTN_HEAD_DIM), 0.1),
        'g_subln': gain(ks[14], ATTN_V_DIM),
        'w_out': nrm(ks[15], (DEPTH, MIX_WIDTH, D_MODEL), MIX_WIDTH ** -0.5),
        'g_memq': gain(ks[16], D_MODEL),
        'g_memkv': gain(ks[17], D_MODEL),
        'wm_q': nrm(ks[18], (DEPTH, D_MODEL, MEM_WIDTH), D_MODEL ** -0.5),
        'wm_kv': nrm(ks[19], (DEPTH, D_MODEL, 2 * MEM_WIDTH), D_MODEL ** -0.5),
        'q_norm_mem': gain(ks[20], MEM_HEAD_DIM),
        'k_norm_mem': gain(ks[21], MEM_HEAD_DIM),
        'wm_o': nrm(ks[22], (DEPTH, MEM_WIDTH, D_MODEL), MEM_WIDTH ** -0.5),
        'g_mlp': gain(ks[23], D_MODEL),
        'w_ff1': nrm(ks[24], (DEPTH, D_MODEL, D_FF), D_MODEL ** -0.5),
        'w_ff2': nrm(ks[25], (DEPTH, D_FF, D_MODEL), D_FF ** -0.5),
    }


def reference(x_prompt, x_sample, mem_prompt, mem_sample, g_mix, w_in, conv_w, conv_b, q_norm, k_norm,
              lambda_q1, lambda_k1, lambda_q2, lambda_k2, g_subln, w_out, g_memq, g_memkv, wm_q, wm_kv,
              q_norm_mem, k_norm_mem, wm_o, g_mlp, w_ff1, w_ff2):
    y_prompt = encoder_trunk(x_prompt, mem_prompt, g_mix, w_in, conv_w, conv_b, q_norm, k_norm,
                             lambda_q1, lambda_k1, lambda_q2, lambda_k2, g_subln, w_out, g_memq, g_memkv,
                             wm_q, wm_kv, q_norm_mem, k_norm_mem, wm_o, g_mlp, w_ff1, w_ff2)
    y_sample = encoder_trunk(x_sample, mem_sample, g_mix, w_in, conv_w, conv_b, q_norm, k_norm,
                             lambda_q1, lambda_k1, lambda_q2, lambda_k2, g_subln, w_out, g_memq, g_memkv,
                             wm_q, wm_kv, q_norm_mem, k_norm_mem, wm_o, g_mlp, w_ff1, w_ff2)
    return (y_prompt, y_sample)
```

```python
import functools
import math

import jax
import jax.numpy as jnp
from jax import lax
from jax.experimental import pallas as pl
from jax.experimental.pallas import tpu as pltpu

LANES = 128
SUBLANES = 8
VMEM_LIMIT_BYTES = 56 * 1024 * 1024

D_MODEL = 1024
CONV_WIDTH = D_MODEL // 2
ATTN_HEADS = 4
ATTN_HEAD_DIM = 64
ATTN_V_DIM = 2 * ATTN_HEAD_DIM
ATTN_WIDTH = ATTN_HEADS * ATTN_V_DIM
QK_WIDTH = ATTN_HEADS * 2 * ATTN_HEAD_DIM
ROPE_DIM = ATTN_HEAD_DIM // 4
ROPE_HALF = ROPE_DIM // 2
ROPE_THETA = 500000.0
MEM_HEADS = 4
MEM_HEAD_DIM = 128
MEM_WIDTH = MEM_HEADS * MEM_HEAD_DIM
D_FF = 4 * D_MODEL
FF_CHUNK = 1024
EPS = 1e-6
LAYER = 0
LAM_INIT = 0.8 - 0.6 * math.exp(-0.3 * LAYER)

COL_B, COL_C, COL_V = 0, CONV_WIDTH, 2 * CONV_WIDTH
COL_Q = 3 * CONV_WIDTH
COL_K = COL_Q + QK_WIDTH
COL_AV = COL_K + QK_WIDTH

BF16 = jnp.bfloat16
F32 = jnp.float32


def _resident(shape):
    nd = len(shape)
    return pl.BlockSpec(shape, lambda *_: (0,) * nd, pipeline_mode=pl.Buffered(1))


def _rms(x, g):
    return x * lax.rsqrt(jnp.mean(x * x, axis=-1, keepdims=True) + EPS) * g


def _dot(a, b):
    return jnp.dot(a, b, preferred_element_type=F32)


def _dot_t(a, b):
    return lax.dot_general(a, b, (((1,), (1,)), ((), ())), preferred_element_type=F32)


def _mem_kv_kernel(mem_ref, g_ref, w_ref, gk_ref, k_ref, v_ref):
    m = _rms(mem_ref[0], g_ref[...]).astype(BF16)
    kv = _dot(m, w_ref[...])
    for h in range(MEM_HEADS):
        kh = kv[:, h * MEM_HEAD_DIM:(h + 1) * MEM_HEAD_DIM]
        k_ref[0, h] = _rms(kh, gk_ref[...]).astype(BF16)
        v_ref[0, h] = kv[:, MEM_WIDTH + h * MEM_HEAD_DIM:
                         MEM_WIDTH + (h + 1) * MEM_HEAD_DIM].astype(BF16)


def _mem_kv(mem, g_memkv, wm_kv, k_norm_mem):
    B, M, D = mem.shape
    out = jax.ShapeDtypeStruct((B, MEM_HEADS, M, MEM_HEAD_DIM), BF16)
    head_spec = pl.BlockSpec((1, MEM_HEADS, M, MEM_HEAD_DIM), lambda b: (b, 0, 0, 0))
    return pl.pallas_call(
        _mem_kv_kernel,
        grid=(B,),
        in_specs=[
            pl.BlockSpec((1, M, D), lambda b: (b, 0, 0)),
            _resident((1, D)),
            _resident((D, 2 * MEM_WIDTH)),
            _resident((1, MEM_HEAD_DIM)),
        ],
        out_specs=[head_spec, head_spec],
        out_shape=[out, out],
        compiler_params=pltpu.CompilerParams(
            dimension_semantics=("arbitrary",), vmem_limit_bytes=VMEM_LIMIT_BYTES),
        name="mem_kv",
    )(mem, g_memkv, wm_kv, k_norm_mem)


def _proj_in_kernel(x_ref, xp_ref, xn_ref, g_ref, w_ref, cw_ref, cb_ref, gq_ref, gk_ref,
                    cos_ref, sa_ref, sb_ref, seg_ref,
                    yc_ref, q_ref, k1_ref, k2_ref, v_ref, *, tm):
    i = pl.program_id(1)
    n_tiles = pl.num_programs(1)
    g = g_ref[...]

    h = _rms(x_ref[0], g).astype(BF16)
    u = _dot(h, w_ref[...])

    z = u[:, COL_C:COL_V] * u[:, COL_V:COL_Q]
    halo = jnp.concatenate([xp_ref[0], xn_ref[0]], axis=0)
    hh = _rms(halo, g).astype(BF16)
    uh = _dot(hh, w_ref[:, COL_C:COL_Q])
    zh = uh[:, :CONV_WIDTH] * uh[:, CONV_WIDTH:]
    z_before = jnp.where(i > 0, zh[SUBLANES - 1:SUBLANES], 0.0)
    z_after = jnp.where(i < n_tiles - 1, zh[SUBLANES:SUBLANES + 1], 0.0)
    row = lax.broadcasted_iota(jnp.int32, (tm, CONV_WIDTH), 0)
    z_prev = jnp.where(row == 0, z_before, pltpu.roll(z, 1, axis=0))
    z_next = jnp.where(row == tm - 1, z_after, pltpu.roll(z, tm - 1, axis=0))
    conv = (z_prev * cw_ref[0:1, :] + z * cw_ref[1:2, :] + z_next * cw_ref[2:3, :]
            + cb_ref[...])
    yc_ref[0] = (u[:, COL_B:COL_C] * conv).astype(BF16)

    cos, sa, sb = cos_ref[...], sa_ref[...], sb_ref[...]
    seg = seg_ref[...]
    lane = lax.broadcasted_iota(jnp.int32, (tm, LANES), 1)
    first = lane < ATTN_HEAD_DIM

    def norm_rope(t, gain):
        ms = _dot((t * t).astype(BF16), seg)
        t = t * lax.rsqrt(ms + EPS) * gain
        return (t * cos + pltpu.roll(t, LANES - ROPE_HALF, axis=1) * sa
                + pltpu.roll(t, ROPE_HALF, axis=1) * sb)

    for hd in range(ATTN_HEADS):
        lo = hd * ATTN_V_DIM
        q = norm_rope(u[:, COL_Q + lo:COL_Q + lo + ATTN_V_DIM], gq_ref[...])
        q_ref[0, hd] = q.astype(BF16)
        k = norm_rope(u[:, COL_K + lo:COL_K + lo + ATTN_V_DIM], gk_ref[...])
        k1_ref[0, hd] = jnp.where(first, k, 0.0).astype(BF16)
        k2_ref[0, hd] = jnp.where(first, 0.0, k).astype(BF16)
        v_ref[0, hd] = u[:, COL_AV + lo:COL_AV + lo + ATTN_V_DIM].astype(BF16)


def _proj_in(x, g_mix, w_in, conv_w, conv_b, gq, gk, cos_t, sa_t, sb_t, seg, *, tm):
    B, S, D = x.shape
    n_tiles = S // tm
    rb = tm // SUBLANES
    n_rb = S // SUBLANES
    head_out = jax.ShapeDtypeStruct((B, ATTN_HEADS, S, ATTN_V_DIM), BF16)
    head_spec = pl.BlockSpec((1, ATTN_HEADS, tm, ATTN_V_DIM), lambda b, i: (b, 0, i, 0))
    tab_spec = pl.BlockSpec((tm, LANES), lambda b, i: (i, 0))
    return pl.pallas_call(
        functools.partial(_proj_in_kernel, tm=tm),
        grid=(B, n_tiles),
        in_specs=[
            pl.BlockSpec((1, tm, D), lambda b, i: (b, i, 0)),
            pl.BlockSpec((1, SUBLANES, D), lambda b, i: (b, jnp.maximum(i * rb - 1, 0), 0)),
            pl.BlockSpec((1, SUBLANES, D), lambda b, i: (b, jnp.minimum((i + 1) * rb, n_rb - 1), 0)),
            _resident((1, D)),
            _resident(w_in.shape),
            _resident(conv_w.shape),
            _resident((1, CONV_WIDTH)),
            _resident((1, LANES)),
            _resident((1, LANES)),
            tab_spec, tab_spec, tab_spec,
            _resident((LANES, LANES)),
        ],
        out_specs=[
            pl.BlockSpec((1, tm, CONV_WIDTH), lambda b, i: (b, i, 0)),
            head_spec, head_spec, head_spec, head_spec,
        ],
        out_shape=[
            jax.ShapeDtypeStruct((B, S, CONV_WIDTH), BF16),
            head_out, head_out, head_out, head_out,
        ],
        compiler_params=pltpu.CompilerParams(
            dimension_semantics=("arbitrary", "arbitrary"), vmem_limit_bytes=VMEM_LIMIT_BYTES),
        name="proj_in",
    )(x, x, x, g_mix, w_in, conv_w, conv_b, gq, gk, cos_t, sa_t, sb_t, seg)


def _diff_attn_kernel(q_ref, k1_ref, k2_ref, v_ref, lq1_ref, lk1_ref, lq2_ref, lk2_ref, gs_ref,
                      o_ref, m_ref, l_ref, acc_ref, *, tk):
    S = k1_ref.shape[2]
    tq = q_ref.shape[2]
    q = q_ref[0, 0]

    m_ref[...] = jnp.full(m_ref.shape, -jnp.inf, F32)
    l_ref[...] = jnp.zeros(l_ref.shape, F32)
    acc_ref[...] = jnp.zeros(acc_ref.shape, F32)

    def body(j, carry):
        ks = pl.ds(pl.multiple_of(j * tk, tk), tk)
        v = v_ref[0, 0, ks, :]
        for c, k_ref in enumerate((k1_ref, k2_ref)):
            s = _dot_t(q, k_ref[0, 0, ks, :])
            m_prev = m_ref[c]
            m_new = jnp.maximum(m_prev, jnp.max(s, axis=-1, keepdims=True))
            alpha = jnp.exp(m_prev - m_new)
            p = jnp.exp(s - m_new)
            l_ref[c] = alpha * l_ref[c] + jnp.sum(p, axis=-1, keepdims=True)
            acc_ref[c] = alpha * acc_ref[c] + _dot(p.astype(BF16), v)
            m_ref[c] = m_new
        return carry

    lax.fori_loop(0, S // tk, body, 0)

    lam = (jnp.exp(jnp.sum(lq1_ref[...] * lk1_ref[...], keepdims=True))
           - jnp.exp(jnp.sum(lq2_ref[...] * lk2_ref[...], keepdims=True)) + LAM_INIT)
    o = acc_ref[0] / l_ref[0] - lam * (acc_ref[1] / l_ref[1])
    o_ref[0] = (_rms(o, gs_ref[...]) * (1.0 - LAM_INIT)).astype(BF16)


def _diff_attn(q, k1, k2, v, lq1, lk1, lq2, lk2, g_subln, *, tq, tk):
    B, H, S, E = q.shape
    kv_spec = pl.BlockSpec((1, 1, S, E), lambda b, h, i: (b, h, 0, 0))
    lam_spec = _resident((1, ATTN_HEAD_DIM))
    return pl.pallas_call(
        functools.partial(_diff_attn_kernel, tk=tk),
        grid=(B, H, S // tq),
        in_specs=[
            pl.BlockSpec((1, 1, tq, E), lambda b, h, i: (b, h, i, 0)),
            kv_spec, kv_spec, kv_spec,
            lam_spec, lam_spec, lam_spec, lam_spec,
            _resident((1, E)),
        ],
        out_specs=pl.BlockSpec((1, tq, E), lambda b, h, i: (b, i, h)),
        out_shape=jax.ShapeDtypeStruct((B, S, H * E), BF16),
        scratch_shapes=[
            pltpu.VMEM((2, tq, 1), F32),
            pltpu.VMEM((2, tq, 1), F32),
            pltpu.VMEM((2, tq, E), F32),
        ],
        compiler_params=pltpu.CompilerParams(
            dimension_semantics=("arbitrary", "arbitrary", "arbitrary"),
            vmem_limit_bytes=VMEM_LIMIT_BYTES),
        name="diff_attn",
    )(q, k1, k2, v, lq1, lk1, lq2, lk2, g_subln)


def _tail_kernel(x_ref, yc_ref, ya_ref, wo_ref, gmq_ref, wq_ref, gqm_ref, km_ref, vm_ref, wmo_ref,
                 gmlp_ref, w1_ref, w2_ref, o_ref):
    x = (x_ref[0] + _dot(yc_ref[0], wo_ref[:CONV_WIDTH, :])
         + _dot(ya_ref[0], wo_ref[CONV_WIDTH:, :]))

    hq = _dot(_rms(x, gmq_ref[...]).astype(BF16), wq_ref[...])
    heads = []
    for h in range(MEM_HEADS):
        qh = hq[:, h * MEM_HEAD_DIM:(h + 1) * MEM_HEAD_DIM]
        qh = (_rms(qh, gqm_ref[...]) * (MEM_HEAD_DIM ** -0.5)).astype(BF16)
        s = _dot_t(qh, km_ref[0, h])
        p = jnp.exp(s - jnp.max(s, axis=-1, keepdims=True))
        oh = _dot(p.astype(BF16), vm_ref[0, h]) / jnp.sum(p, axis=-1, keepdims=True)
        heads.append(oh.astype(BF16))
    x = x + _dot(jnp.concatenate(heads, axis=-1), wmo_ref[...])

    hn = _rms(x, gmlp_ref[...]).astype(BF16)
    y = x
    for c in range(D_FF // FF_CHUNK):
        cs = slice(c * FF_CHUNK, (c + 1) * FF_CHUNK)
        hf = jnp.maximum(_dot(hn, w1_ref[:, cs]), 0.0)
        y = y + _dot((hf * hf).astype(BF16), w2_ref[cs, :])
    o_ref[0] = y


def _tail(x, y_conv, y_attn, w_out, g_memq, wm_q, q_norm_mem, k_mem, v_mem, wm_o, g_mlp,
          w_ff1, w_ff2, *, tm):
    B, S, D = x.shape
    M = k_mem.shape[2]
    tok = lambda w: pl.BlockSpec((1, tm, w), lambda b, i: (b, i, 0))
    mem_spec = pl.BlockSpec((1, MEM_HEADS, M, MEM_HEAD_DIM), lambda b, i: (b, 0, 0, 0))
    return pl.pallas_call(
        _tail_kernel,
        grid=(B, S // tm),
        in_specs=[
            tok(D), tok(CONV_WIDTH), tok(ATTN_WIDTH),
            _resident(w_out.shape),
            _resident((1, D)),
            _resident(wm_q.shape),
            _resident((1, MEM_HEAD_DIM)),
            mem_spec, mem_spec,
            _resident(wm_o.shape),
            _resident((1, D)),
            _resident(w_ff1.shape),
            _resident(w_ff2.shape),
        ],
        out_specs=tok(D),
        out_shape=jax.ShapeDtypeStruct((B, S, D), F32),
        compiler_params=pltpu.CompilerParams(
            dimension_semantics=("arbitrary", "arbitrary"), vmem_limit_bytes=VMEM_LIMIT_BYTES),
        name="tail",
    )(x, y_conv, y_attn, w_out, g_memq, wm_q, q_norm_mem, k_mem, v_mem, wm_o, g_mlp, w_ff1, w_ff2)


def _rope_tables(S):
    pos = jnp.arange(S, dtype=F32)
    inv_freq = ROPE_THETA ** (-jnp.arange(0, ROPE_DIM, 2, dtype=F32) / ROPE_DIM)
    ang = pos[:, None] * inv_freq[None, :]
    cos, sin = jnp.cos(ang), jnp.sin(ang)
    one = jnp.ones((S, ATTN_HEAD_DIM - ROPE_DIM), F32)
    zero = jnp.zeros((S, ATTN_HEAD_DIM - ROPE_DIM), F32)
    zh = jnp.zeros((S, ROPE_HALF), F32)
    cos_c = jnp.concatenate([cos, cos, one], axis=-1)
    sa_c = jnp.concatenate([-sin, zh, zero], axis=-1)
    sb_c = jnp.concatenate([zh, sin, zero], axis=-1)
    tile2 = lambda t: jnp.concatenate([t, t], axis=-1)
    return tile2(cos_c), tile2(sa_c), tile2(sb_c)


def _segment_mean_matrix():
    lane = jnp.arange(LANES)
    same = (lane[:, None] // ATTN_HEAD_DIM) == (lane[None, :] // ATTN_HEAD_DIM)
    return jnp.where(same, 1.0 / ATTN_HEAD_DIM, 0.0).astype(BF16)


def _trunk(x, mem, p, *, tm, tq, tk):
    B, S, _ = x.shape
    k_mem, v_mem = _mem_kv(mem, p["g_memkv"], p["wm_kv"], p["k_norm_mem"])
    cos_t, sa_t, sb_t = _rope_tables(S)
    y_conv, q, k1, k2, v = _proj_in(
        x, p["g_mix"], p["w_in"], p["conv_w"], p["conv_b"], p["gq"], p["gk"],
        cos_t, sa_t, sb_t, p["seg"], tm=tm)
    y_attn = _diff_attn(q, k1, k2, v, p["lq1"], p["lk1"], p["lq2"], p["lk2"], p["g_subln"],
                        tq=tq, tk=tk)
    return _tail(x, y_conv, y_attn, p["w_out"], p["g_memq"], p["wm_q"], p["q_norm_mem"],
                 k_mem, v_mem, p["wm_o"], p["g_mlp"], p["w_ff1"], p["w_ff2"], tm=tm)


def kernel(x_prompt, x_sample, mem_prompt, mem_sample, g_mix, w_in, conv_w, conv_b, q_norm, k_norm, lambda_q1, lambda_k1, lambda_q2, lambda_k2, g_subln, w_out, g_memq, g_memkv, wm_q, wm_kv, q_norm_mem, k_norm_mem, wm_o, g_mlp, w_ff1, w_ff2):
    l = LAYER
    tile2 = lambda t: jnp.concatenate([t, t], axis=-1)
    p = {
        "g_mix": g_mix[l][None], "w_in": w_in[l].astype(BF16),
        "conv_w": conv_w[l], "conv_b": conv_b[l][None],
        "gq": tile2(q_norm[l])[None] * (ATTN_HEAD_DIM ** -0.5), "gk": tile2(k_norm[l])[None],
        "seg": _segment_mean_matrix(),
        "lq1": lambda_q1[l][None], "lk1": lambda_k1[l][None],
        "lq2": lambda_q2[l][None], "lk2": lambda_k2[l][None],
        "g_subln": g_subln[l][None], "w_out": w_out[l].astype(BF16),
        "g_memq": g_memq[l][None], "g_memkv": g_memkv[l][None],
        "wm_q": wm_q[l].astype(BF16), "wm_kv": wm_kv[l].astype(BF16),
        "q_norm_mem": q_norm_mem[l][None], "k_norm_mem": k_norm_mem[l][None],
        "wm_o": wm_o[l].astype(BF16), "g_mlp": g_mlp[l][None],
        "w_ff1": w_ff1[l].astype(BF16), "w_ff2": w_ff2[l].astype(BF16),
    }
    tiles = dict(tm=512, tq=512, tk=512)
    y_prompt = _trunk(x_prompt, mem_prompt, p, **tiles)
    y_sample = _trunk(x_sample, mem_sample, p, **tiles)
    return (y_prompt, y_sample)
```

```python
import functools
import math

import jax
import jax.numpy as jnp
from jax import lax
from jax.experimental import pallas as pl
from jax.experimental.pallas import tpu as pltpu

LANES = 128
SUBLANES = 8
VMEM_LIMIT_BYTES = 56 * 1024 * 1024

D_MODEL = 1024
CONV_WIDTH = D_MODEL // 2
ATTN_HEADS = 4
ATTN_HEAD_DIM = 64
ATTN_V_DIM = 2 * ATTN_HEAD_DIM
ATTN_WIDTH = ATTN_HEADS * ATTN_V_DIM
QK_WIDTH = ATTN_HEADS * 2 * ATTN_HEAD_DIM
ROPE_DIM = ATTN_HEAD_DIM // 4
ROPE_HALF = ROPE_DIM // 2
ROPE_THETA = 500000.0
MEM_HEADS = 4
MEM_HEAD_DIM = 128
MEM_WIDTH = MEM_HEADS * MEM_HEAD_DIM
D_FF = 4 * D_MODEL
FF_CHUNK = 1024
EPS = 1e-6
LAYER = 0
LAM_INIT = 0.8 - 0.6 * math.exp(-0.3 * LAYER)
SCORE_BOUND_MAX = 60.0

COL_B, COL_C, COL_V = 0, CONV_WIDTH, 2 * CONV_WIDTH
COL_Q = 3 * CONV_WIDTH
COL_K = COL_Q + QK_WIDTH
COL_AV = COL_K + QK_WIDTH

BF16 = jnp.bfloat16
F32 = jnp.float32


def _resident(shape):
    nd = len(shape)
    return pl.BlockSpec(shape, lambda *_: (0,) * nd, pipeline_mode=pl.Buffered(1))


def _rms(x, g):
    return x * lax.rsqrt(jnp.mean(x * x, axis=-1, keepdims=True) + EPS) * g


def _dot(a, b):
    return jnp.dot(a, b, preferred_element_type=F32)


def _dot_t(a, b):
    return lax.dot_general(a, b, (((1,), (1,)), ((), ())), preferred_element_type=F32)


def _mem_kv_kernel(mem_ref, g_ref, w_ref, gk_ref, k_ref, v_ref):
    m = _rms(mem_ref[0], g_ref[...]).astype(BF16)
    kv = _dot(m, w_ref[...])
    for h in range(MEM_HEADS):
        kh = kv[:, h * MEM_HEAD_DIM:(h + 1) * MEM_HEAD_DIM]
        k_ref[0, h] = _rms(kh, gk_ref[...]).astype(BF16)
        v_ref[0, h] = kv[:, MEM_WIDTH + h * MEM_HEAD_DIM:
                         MEM_WIDTH + (h + 1) * MEM_HEAD_DIM].astype(BF16)


def _mem_kv(mem, g_memkv, wm_kv, k_norm_mem):
    B, M, D = mem.shape
    out = jax.ShapeDtypeStruct((B, MEM_HEADS, M, MEM_HEAD_DIM), BF16)
    head_spec = pl.BlockSpec((1, MEM_HEADS, M, MEM_HEAD_DIM), lambda b: (b, 0, 0, 0))
    return pl.pallas_call(
        _mem_kv_kernel,
        grid=(B,),
        in_specs=[
            pl.BlockSpec((1, M, D), lambda b: (b, 0, 0)),
            _resident((1, D)),
            _resident((D, 2 * MEM_WIDTH)),
            _resident((1, MEM_HEAD_DIM)),
        ],
        out_specs=[head_spec, head_spec],
        out_shape=[out, out],
        compiler_params=pltpu.CompilerParams(
            dimension_semantics=("arbitrary",), vmem_limit_bytes=VMEM_LIMIT_BYTES),
        name="mem_kv",
    )(mem, g_memkv, wm_kv, k_norm_mem)


def _proj_in_kernel(x_ref, xp_ref, xn_ref, g_ref, w_ref, cw_ref, cb_ref, gq_ref, gk_ref,
                    cos_ref, sa_ref, sb_ref, seg_ref,
                    yc_ref, q_ref, k1_ref, k2_ref, v_ref, *, tm):
    i = pl.program_id(1)
    n_tiles = pl.num_programs(1)
    g = g_ref[...]

    h = _rms(x_ref[0], g).astype(BF16)
    u = _dot(h, w_ref[...])

    z = u[:, COL_C:COL_V] * u[:, COL_V:COL_Q]
    halo = jnp.concatenate([xp_ref[0], xn_ref[0]], axis=0)
    hh = _rms(halo, g).astype(BF16)
    uh = _dot(hh, w_ref[:, COL_C:COL_Q])
    zh = uh[:, :CONV_WIDTH] * uh[:, CONV_WIDTH:]
    z_before = jnp.where(i > 0, zh[SUBLANES - 1:SUBLANES], 0.0)
    z_after = jnp.where(i < n_tiles - 1, zh[SUBLANES:SUBLANES + 1], 0.0)
    row = lax.broadcasted_iota(jnp.int32, (tm, CONV_WIDTH), 0)
    z_prev = jnp.where(row == 0, z_before, pltpu.roll(z, 1, axis=0))
    z_next = jnp.where(row == tm - 1, z_after, pltpu.roll(z, tm - 1, axis=0))
    conv = (z_prev * cw_ref[0:1, :] + z * cw_ref[1:2, :] + z_next * cw_ref[2:3, :]
            + cb_ref[...])
    yc_ref[0] = (u[:, COL_B:COL_C] * conv).astype(BF16)

    cos, sa, sb = cos_ref[...], sa_ref[...], sb_ref[...]
    seg = seg_ref[...]
    lane = lax.broadcasted_iota(jnp.int32, (tm, LANES), 1)
    first = lane < ATTN_HEAD_DIM

    def norm_rope(t, gain):
        ms = _dot((t * t).astype(BF16), seg)
        t = t * lax.rsqrt(ms + EPS) * gain
        return (t * cos + pltpu.roll(t, LANES - ROPE_HALF, axis=1) * sa
                + pltpu.roll(t, ROPE_HALF, axis=1) * sb)

    for hd in range(ATTN_HEADS):
        lo = hd * ATTN_V_DIM
        q = norm_rope(u[:, COL_Q + lo:COL_Q + lo + ATTN_V_DIM], gq_ref[...])
        q_ref[0, hd] = q.astype(BF16)
        k = norm_rope(u[:, COL_K + lo:COL_K + lo + ATTN_V_DIM], gk_ref[...])
        k1_ref[0, hd] = jnp.where(first, k, 0.0).astype(BF16)
        k2_ref[0, hd] = jnp.where(first, 0.0, k).astype(BF16)
        v_ref[0, hd, :, :ATTN_V_DIM] = u[:, COL_AV + lo:COL_AV + lo + ATTN_V_DIM].astype(BF16)
        v_ref[0, hd, :, ATTN_V_DIM:] = jnp.ones((tm, ATTN_V_DIM), BF16)


def _proj_in(x, g_mix, w_in, conv_w, conv_b, gq, gk, cos_t, sa_t, sb_t, seg, *, tm):
    B, S, D = x.shape
    n_tiles = S // tm
    rb = tm // SUBLANES
    n_rb = S // SUBLANES
    head_out = jax.ShapeDtypeStruct((B, ATTN_HEADS, S, ATTN_V_DIM), BF16)
    head_spec = pl.BlockSpec((1, ATTN_HEADS, tm, ATTN_V_DIM), lambda b, i: (b, 0, i, 0))
    v1_out = jax.ShapeDtypeStruct((B, ATTN_HEADS, S, 2 * ATTN_V_DIM), BF16)
    v1_spec = pl.BlockSpec((1, ATTN_HEADS, tm, 2 * ATTN_V_DIM), lambda b, i: (b, 0, i, 0))
    tab_spec = pl.BlockSpec((tm, LANES), lambda b, i: (i, 0))
    return pl.pallas_call(
        functools.partial(_proj_in_kernel, tm=tm),
        grid=(B, n_tiles),
        in_specs=[
            pl.BlockSpec((1, tm, D), lambda b, i: (b, i, 0)),
            pl.BlockSpec((1, SUBLANES, D), lambda b, i: (b, jnp.maximum(i * rb - 1, 0), 0)),
            pl.BlockSpec((1, SUBLANES, D), lambda b, i: (b, jnp.minimum((i + 1) * rb, n_rb - 1), 0)),
            _resident((1, D)),
            _resident(w_in.shape),
            _resident(conv_w.shape),
            _resident((1, CONV_WIDTH)),
            _resident((1, LANES)),
            _resident((1, LANES)),
            tab_spec, tab_spec, tab_spec,
            _resident((LANES, LANES)),
        ],
        out_specs=[
            pl.BlockSpec((1, tm, CONV_WIDTH), lambda b, i: (b, i, 0)),
            head_spec, head_spec, head_spec, v1_spec,
        ],
        out_shape=[
            jax.ShapeDtypeStruct((B, S, CONV_WIDTH), BF16),
            head_out, head_out, head_out, v1_out,
        ],
        compiler_params=pltpu.CompilerParams(
            dimension_semantics=("arbitrary", "arbitrary"), vmem_limit_bytes=VMEM_LIMIT_BYTES),
        name="proj_in",
    )(x, x, x, g_mix, w_in, conv_w, conv_b, gq, gk, cos_t, sa_t, sb_t, seg)


def _attn_finish(o1, l1, o2, l2, lq1_ref, lk1_ref, lq2_ref, lk2_ref, gs_ref, o_ref):
    lam = (jnp.exp(jnp.sum(lq1_ref[...] * lk1_ref[...], keepdims=True))
           - jnp.exp(jnp.sum(lq2_ref[...] * lk2_ref[...], keepdims=True)) + LAM_INIT)
    o = o1 / l1 - lam * (o2 / l2)
    o_ref[0] = (_rms(o, gs_ref[...]) * (1.0 - LAM_INIT)).astype(BF16)


def _diff_attn_online_kernel(q_ref, k1_ref, k2_ref, v_ref, lq1_ref, lk1_ref, lq2_ref, lk2_ref,
                             gs_ref, o_ref, m_ref, l_ref, acc_ref, *, tk):
    S = k1_ref.shape[2]
    q = q_ref[0, 0]

    m_ref[...] = jnp.full(m_ref.shape, -jnp.inf, F32)
    l_ref[...] = jnp.zeros(l_ref.shape, F32)
    acc_ref[...] = jnp.zeros(acc_ref.shape, F32)

    def body(j, carry):
        ks = pl.ds(pl.multiple_of(j * tk, tk), tk)
        v = v_ref[0, 0, ks, :ATTN_V_DIM]
        for c, k_ref in enumerate((k1_ref, k2_ref)):
            s = _dot_t(q, k_ref[0, 0, ks, :])
            m_prev = m_ref[c]
            m_new = jnp.maximum(m_prev, jnp.max(s, axis=-1, keepdims=True))
            alpha = jnp.exp(m_prev - m_new)
            p = jnp.exp(s - m_new)
            l_ref[c] = alpha * l_ref[c] + jnp.sum(p, axis=-1, keepdims=True)
            acc_ref[c] = alpha * acc_ref[c] + _dot(p.astype(BF16), v)
            m_ref[c] = m_new
        return carry

    lax.fori_loop(0, S // tk, body, 0)
    _attn_finish(acc_ref[0], l_ref[0], acc_ref[1], l_ref[1],
                 lq1_ref, lk1_ref, lq2_ref, lk2_ref, gs_ref, o_ref)


def _diff_attn_bounded_kernel(q_ref, k1_ref, k2_ref, v_ref, lq1_ref, lk1_ref, lq2_ref, lk2_ref,
                              gs_ref, o_ref, acc_ref, *, tk):
    S = k1_ref.shape[2]
    q = q_ref[0, 0]
    acc_ref[...] = jnp.zeros(acc_ref.shape, F32)

    def body(j, carry):
        ks = pl.ds(pl.multiple_of(j * tk, tk), tk)
        v1 = v_ref[0, 0, ks, :]
        for c, k_ref in enumerate((k1_ref, k2_ref)):
            s = _dot_t(q, k_ref[0, 0, ks, :])
            acc_ref[c] += _dot(jnp.exp(s).astype(BF16), v1)
        return carry

    lax.fori_loop(0, S // tk, body, 0)
    _attn_finish(acc_ref[0, :, :ATTN_V_DIM], acc_ref[0, :, ATTN_V_DIM:],
                 acc_ref[1, :, :ATTN_V_DIM], acc_ref[1, :, ATTN_V_DIM:],
                 lq1_ref, lk1_ref, lq2_ref, lk2_ref, gs_ref, o_ref)


def _diff_attn(q, k1, k2, v1, lq1, lk1, lq2, lk2, g_subln, *, bounded, tq, tk):
    B, H, S, E = q.shape
    kv_spec = pl.BlockSpec((1, 1, S, E), lambda b, h, i: (b, h, 0, 0))
    v1_spec = pl.BlockSpec((1, 1, S, 2 * E), lambda b, h, i: (b, h, 0, 0))
    lam_spec = _resident((1, ATTN_HEAD_DIM))
    if bounded:
        body = functools.partial(_diff_attn_bounded_kernel, tk=tk)
        scratch = [pltpu.VMEM((2, tq, 2 * E), F32)]
    else:
        body = functools.partial(_diff_attn_online_kernel, tk=tk)
        scratch = [pltpu.VMEM((2, tq, 1), F32), pltpu.VMEM((2, tq, 1), F32),
                   pltpu.VMEM((2, tq, E), F32)]
    return pl.pallas_call(
        body,
        grid=(B, H, S // tq),
        in_specs=[
            pl.BlockSpec((1, 1, tq, E), lambda b, h, i: (b, h, i, 0)),
            kv_spec, kv_spec, v1_spec,
            lam_spec, lam_spec, lam_spec, lam_spec,
            _resident((1, E)),
        ],
        out_specs=pl.BlockSpec((1, tq, E), lambda b, h, i: (b, i, h)),
        out_shape=jax.ShapeDtypeStruct((B, S, H * E), BF16),
        scratch_shapes=scratch,
        compiler_params=pltpu.CompilerParams(
            dimension_semantics=("arbitrary", "arbitrary", "arbitrary"),
            vmem_limit_bytes=VMEM_LIMIT_BYTES),
        name="diff_attn_bounded" if bounded else "diff_attn_online",
    )(q, k1, k2, v1, lq1, lk1, lq2, lk2, g_subln)


def _tail_kernel(x_ref, yc_ref, ya_ref, wo_ref, gmq_ref, wq_ref, gqm_ref, km_ref, vm_ref, wmo_ref,
                 gmlp_ref, w1_ref, w2_ref, o_ref):
    x = (x_ref[0] + _dot(yc_ref[0], wo_ref[:CONV_WIDTH, :])
         + _dot(ya_ref[0], wo_ref[CONV_WIDTH:, :]))

    hq = _dot(_rms(x, gmq_ref[...]).astype(BF16), wq_ref[...])
    heads = []
    for h in range(MEM_HEADS):
        qh = hq[:, h * MEM_HEAD_DIM:(h + 1) * MEM_HEAD_DIM]
        qh = (_rms(qh, gqm_ref[...]) * (MEM_HEAD_DIM ** -0.5)).astype(BF16)
        s = _dot_t(qh, km_ref[0, h])
        p = jnp.exp(s - jnp.max(s, axis=-1, keepdims=True))
        oh = _dot(p.astype(BF16), vm_ref[0, h]) / jnp.sum(p, axis=-1, keepdims=True)
        heads.append(oh.astype(BF16))
    x = x + _dot(jnp.concatenate(heads, axis=-1), wmo_ref[...])

    hn = _rms(x, gmlp_ref[...]).astype(BF16)
    y = x
    for c in range(D_FF // FF_CHUNK):
        cs = slice(c * FF_CHUNK, (c + 1) * FF_CHUNK)
        hf = jnp.maximum(_dot(hn, w1_ref[:, cs]), 0.0)
        y = y + _dot((hf * hf).astype(BF16), w2_ref[cs, :])
    o_ref[0] = y


def _tail(x, y_conv, y_attn, w_out, g_memq, wm_q, q_norm_mem, k_mem, v_mem, wm_o, g_mlp,
          w_ff1, w_ff2, *, tm):
    B, S, D = x.shape
    M = k_mem.shape[2]
    tok = lambda w: pl.BlockSpec((1, tm, w), lambda b, i: (b, i, 0))
    mem_spec = pl.BlockSpec((1, MEM_HEADS, M, MEM_HEAD_DIM), lambda b, i: (b, 0, 0, 0))
    return pl.pallas_call(
        _tail_kernel,
        grid=(B, S // tm),
        in_specs=[
            tok(D), tok(CONV_WIDTH), tok(ATTN_WIDTH),
            _resident(w_out.shape),
            _resident((1, D)),
            _resident(wm_q.shape),
            _resident((1, MEM_HEAD_DIM)),
            mem_spec, mem_spec,
            _resident(wm_o.shape),
            _resident((1, D)),
            _resident(w_ff1.shape),
            _resident(w_ff2.shape),
        ],
        out_specs=tok(D),
        out_shape=jax.ShapeDtypeStruct((B, S, D), F32),
        compiler_params=pltpu.CompilerParams(
            dimension_semantics=("arbitrary", "arbitrary"), vmem_limit_bytes=VMEM_LIMIT_BYTES),
        name="tail",
    )(x, y_conv, y_attn, w_out, g_memq, wm_q, q_norm_mem, k_mem, v_mem, wm_o, g_mlp, w_ff1, w_ff2)


def _rope_tables(S):
    pos = jnp.arange(S, dtype=F32)
    inv_freq = ROPE_THETA ** (-jnp.arange(0, ROPE_DIM, 2, dtype=F32) / ROPE_DIM)
    ang = pos[:, None] * inv_freq[None, :]
    cos, sin = jnp.cos(ang), jnp.sin(ang)
    one = jnp.ones((S, ATTN_HEAD_DIM - ROPE_DIM), F32)
    zero = jnp.zeros((S, ATTN_HEAD_DIM - ROPE_DIM), F32)
    zh = jnp.zeros((S, ROPE_HALF), F32)
    cos_c = jnp.concatenate([cos, cos, one], axis=-1)
    sa_c = jnp.concatenate([-sin, zh, zero], axis=-1)
    sb_c = jnp.concatenate([zh, sin, zero], axis=-1)
    tile2 = lambda t: jnp.concatenate([t, t], axis=-1)
    return tile2(cos_c), tile2(sa_c), tile2(sb_c)


def _segment_mean_matrix():
    lane = jnp.arange(LANES)
    same = (lane[:, None] // ATTN_HEAD_DIM) == (lane[None, :] // ATTN_HEAD_DIM)
    return jnp.where(same, 1.0 / ATTN_HEAD_DIM, 0.0).astype(BF16)


def _trunk(x, mem, p, *, tm, tq, tk, tk_bounded):
    B, S, _ = x.shape
    k_mem, v_mem = _mem_kv(mem, p["g_memkv"], p["wm_kv"], p["k_norm_mem"])
    cos_t, sa_t, sb_t = _rope_tables(S)
    y_conv, q, k1, k2, v = _proj_in(
        x, p["g_mix"], p["w_in"], p["conv_w"], p["conv_b"], p["gq"], p["gk"],
        cos_t, sa_t, sb_t, p["seg"], tm=tm)
    attn = functools.partial(_diff_attn, q, k1, k2, v, p["lq1"], p["lk1"], p["lq2"], p["lk2"],
                             p["g_subln"])
    y_attn = lax.cond(p["score_bound"] <= SCORE_BOUND_MAX,
                      lambda: attn(bounded=True, tq=tq, tk=tk_bounded),
                      lambda: attn(bounded=False, tq=tq, tk=tk))
    return _tail(x, y_conv, y_attn, p["w_out"], p["g_memq"], p["wm_q"], p["q_norm_mem"],
                 k_mem, v_mem, p["wm_o"], p["g_mlp"], p["w_ff1"], p["w_ff2"], tm=tm)


def kernel(x_prompt, x_sample, mem_prompt, mem_sample, g_mix, w_in, conv_w, conv_b, q_norm, k_norm, lambda_q1, lambda_k1, lambda_q2, lambda_k2, g_subln, w_out, g_memq, g_memkv, wm_q, wm_kv, q_norm_mem, k_norm_mem, wm_o, g_mlp, w_ff1, w_ff2):
    l = LAYER
    tile2 = lambda t: jnp.concatenate([t, t], axis=-1)
    p = {
        "g_mix": g_mix[l][None], "w_in": w_in[l].astype(BF16),
        "conv_w": conv_w[l], "conv_b": conv_b[l][None],
        "gq": tile2(q_norm[l])[None] * (ATTN_HEAD_DIM ** -0.5), "gk": tile2(k_norm[l])[None],
        "seg": _segment_mean_matrix(),
        "score_bound": (1.02 * ATTN_HEAD_DIM ** 0.5 * jnp.max(jnp.abs(q_norm[l]))
                        * jnp.max(jnp.abs(k_norm[l]))),
        "lq1": lambda_q1[l][None], "lk1": lambda_k1[l][None],
        "lq2": lambda_q2[l][None], "lk2": lambda_k2[l][None],
        "g_subln": g_subln[l][None], "w_out": w_out[l].astype(BF16),
        "g_memq": g_memq[l][None], "g_memkv": g_memkv[l][None],
        "wm_q": wm_q[l].astype(BF16), "wm_kv": wm_kv[l].astype(BF16),
        "q_norm_mem": q_norm_mem[l][None], "k_norm_mem": k_norm_mem[l][None],
        "wm_o": wm_o[l].astype(BF16), "g_mlp": g_mlp[l][None],
        "w_ff1": w_ff1[l].astype(BF16), "w_ff2": w_ff2[l].astype(BF16),
    }
    tiles = dict(tm=512, tq=512, tk=512, tk_bounded=1024)
    y_prompt = _trunk(x_prompt, mem_prompt, p, **tiles)
    y_sample = _trunk(x_sample, mem_sample, p, **tiles)
    return (y_prompt, y_sample)
```

```python
import functools
import math

import jax
import jax.numpy as jnp
from jax import lax
from jax.experimental import pallas as pl
from jax.experimental.pallas import tpu as pltpu

LANES = 128
SUBLANES = 8
VMEM_LIMIT_BYTES = 56 * 1024 * 1024

D_MODEL = 1024
CONV_WIDTH = D_MODEL // 2
ATTN_HEADS = 4
ATTN_HEAD_DIM = 64
ATTN_V_DIM = 2 * ATTN_HEAD_DIM
ATTN_WIDTH = ATTN_HEADS * ATTN_V_DIM
QK_WIDTH = ATTN_HEADS * 2 * ATTN_HEAD_DIM
ROPE_DIM = ATTN_HEAD_DIM // 4
ROPE_HALF = ROPE_DIM // 2
ROPE_THETA = 500000.0
MEM_HEADS = 4
MEM_HEAD_DIM = 128
MEM_WIDTH = MEM_HEADS * MEM_HEAD_DIM
D_FF = 4 * D_MODEL
FF_CHUNK = 1024
EPS = 1e-6
LAYER = 0
LAM_INIT = 0.8 - 0.6 * math.exp(-0.3 * LAYER)
SCORE_BOUND_MAX = 60.0

COL_B, COL_C, COL_V = 0, CONV_WIDTH, 2 * CONV_WIDTH
COL_Q = 3 * CONV_WIDTH
COL_K = COL_Q + QK_WIDTH
COL_AV = COL_K + QK_WIDTH

BF16 = jnp.bfloat16
F32 = jnp.float32


def _resident(shape):
    nd = len(shape)
    return pl.BlockSpec(shape, lambda *_: (0,) * nd, pipeline_mode=pl.Buffered(1))


def _rms(x, g):
    return x * lax.rsqrt(jnp.mean(x * x, axis=-1, keepdims=True) + EPS) * g


def _dot(a, b):
    return jnp.dot(a, b, preferred_element_type=F32)


def _dot_t(a, b):
    return lax.dot_general(a, b, (((1,), (1,)), ((), ())), preferred_element_type=F32)


def _mem_kv_kernel(mem_ref, g_ref, w_ref, gk_ref, k_ref, v_ref):
    m = _rms(mem_ref[0], g_ref[...]).astype(BF16)
    kv = _dot(m, w_ref[...])
    for h in range(MEM_HEADS):
        kh = kv[:, h * MEM_HEAD_DIM:(h + 1) * MEM_HEAD_DIM]
        k_ref[0, h] = _rms(kh, gk_ref[...]).astype(BF16)
        v_ref[0, h] = kv[:, MEM_WIDTH + h * MEM_HEAD_DIM:
                         MEM_WIDTH + (h + 1) * MEM_HEAD_DIM].astype(BF16)


def _mem_kv(mem, g_memkv, wm_kv, k_norm_mem):
    B, M, D = mem.shape
    out = jax.ShapeDtypeStruct((B, MEM_HEADS, M, MEM_HEAD_DIM), BF16)
    head_spec = pl.BlockSpec((1, MEM_HEADS, M, MEM_HEAD_DIM), lambda b: (b, 0, 0, 0))
    return pl.pallas_call(
        _mem_kv_kernel,
        grid=(B,),
        in_specs=[
            pl.BlockSpec((1, M, D), lambda b: (b, 0, 0)),
            _resident((1, D)),
            _resident((D, 2 * MEM_WIDTH)),
            _resident((1, MEM_HEAD_DIM)),
        ],
        out_specs=[head_spec, head_spec],
        out_shape=[out, out],
        compiler_params=pltpu.CompilerParams(
            dimension_semantics=("arbitrary",), vmem_limit_bytes=VMEM_LIMIT_BYTES),
        name="mem_kv",
    )(mem, g_memkv, wm_kv, k_norm_mem)


def _proj_in_kernel(x_ref, xp_ref, xn_ref, g_ref, w_ref, cw_ref, cb_ref, gq_ref, gk_ref,
                    cos_ref, sa_ref, sb_ref, seg_ref,
                    yc_ref, q_ref, k1_ref, k2_ref, v_ref, *, tm):
    i = pl.program_id(1)
    n_tiles = pl.num_programs(1)
    g = g_ref[...]

    h = _rms(x_ref[0], g).astype(BF16)
    u = _dot(h, w_ref[...])

    z = u[:, COL_C:COL_V] * u[:, COL_V:COL_Q]
    halo = jnp.concatenate([xp_ref[0], xn_ref[0]], axis=0)
    hh = _rms(halo, g).astype(BF16)
    uh = _dot(hh, w_ref[:, COL_C:COL_Q])
    zh = uh[:, :CONV_WIDTH] * uh[:, CONV_WIDTH:]
    z_before = jnp.where(i > 0, zh[SUBLANES - 1:SUBLANES], 0.0)
    z_after = jnp.where(i < n_tiles - 1, zh[SUBLANES:SUBLANES + 1], 0.0)
    row = lax.broadcasted_iota(jnp.int32, (tm, CONV_WIDTH), 0)
    z_prev = jnp.where(row == 0, z_before, pltpu.roll(z, 1, axis=0))
    z_next = jnp.where(row == tm - 1, z_after, pltpu.roll(z, tm - 1, axis=0))
    conv = (z_prev * cw_ref[0:1, :] + z * cw_ref[1:2, :] + z_next * cw_ref[2:3, :]
            + cb_ref[...])
    yc_ref[0] = (u[:, COL_B:COL_C] * conv).astype(BF16)

    cos, sa, sb = cos_ref[...], sa_ref[...], sb_ref[...]
    seg = seg_ref[...]
    lane = lax.broadcasted_iota(jnp.int32, (tm, LANES), 1)
    first = lane < ATTN_HEAD_DIM

    def norm_rope(t, gain):
        ms = _dot((t * t).astype(BF16), seg)
        t = t * lax.rsqrt(ms + EPS) * gain
        return (t * cos + pltpu.roll(t, LANES - ROPE_HALF, axis=1) * sa
                + pltpu.roll(t, ROPE_HALF, axis=1) * sb)

    for hd in range(ATTN_HEADS):
        lo = hd * ATTN_V_DIM
        q = norm_rope(u[:, COL_Q + lo:COL_Q + lo + ATTN_V_DIM], gq_ref[...])
        q_ref[0, hd] = q.astype(BF16)
        k = norm_rope(u[:, COL_K + lo:COL_K + lo + ATTN_V_DIM], gk_ref[...])
        k1_ref[0, hd] = jnp.where(first, k, 0.0).astype(BF16)
        k2_ref[0, hd] = jnp.where(first, 0.0, k).astype(BF16)
        v_ref[0, hd, :, :ATTN_V_DIM] = u[:, COL_AV + lo:COL_AV + lo + ATTN_V_DIM].astype(BF16)
        v_ref[0, hd, :, ATTN_V_DIM:] = jnp.ones((tm, ATTN_V_DIM), BF16)


def _proj_in(x, g_mix, w_in, conv_w, conv_b, gq, gk, cos_t, sa_t, sb_t, seg, *, tm):
    B, S, D = x.shape
    n_tiles = S // tm
    rb = tm // SUBLANES
    n_rb = S // SUBLANES
    head_out = jax.ShapeDtypeStruct((B, ATTN_HEADS, S, ATTN_V_DIM), BF16)
    head_spec = pl.BlockSpec((1, ATTN_HEADS, tm, ATTN_V_DIM), lambda b, i: (b, 0, i, 0))
    v1_out = jax.ShapeDtypeStruct((B, ATTN_HEADS, S, 2 * ATTN_V_DIM), BF16)
    v1_spec = pl.BlockSpec((1, ATTN_HEADS, tm, 2 * ATTN_V_DIM), lambda b, i: (b, 0, i, 0))
    tab_spec = pl.BlockSpec((tm, LANES), lambda b, i: (i, 0))
    return pl.pallas_call(
        functools.partial(_proj_in_kernel, tm=tm),
        grid=(B, n_tiles),
        in_specs=[
            pl.BlockSpec((1, tm, D), lambda b, i: (b, i, 0)),
            pl.BlockSpec((1, SUBLANES, D), lambda b, i: (b, jnp.maximum(i * rb - 1, 0), 0)),
            pl.BlockSpec((1, SUBLANES, D), lambda b, i: (b, jnp.minimum((i + 1) * rb, n_rb - 1), 0)),
            _resident((1, D)),
            _resident(w_in.shape),
            _resident(conv_w.shape),
            _resident((1, CONV_WIDTH)),
            _resident((1, LANES)),
            _resident((1, LANES)),
            tab_spec, tab_spec, tab_spec,
            _resident((LANES, LANES)),
        ],
        out_specs=[
            pl.BlockSpec((1, tm, CONV_WIDTH), lambda b, i: (b, i, 0)),
            head_spec, head_spec, head_spec, v1_spec,
        ],
        out_shape=[
            jax.ShapeDtypeStruct((B, S, CONV_WIDTH), BF16),
            head_out, head_out, head_out, v1_out,
        ],
        compiler_params=pltpu.CompilerParams(
            dimension_semantics=("arbitrary", "arbitrary"), vmem_limit_bytes=VMEM_LIMIT_BYTES),
        name="proj_in",
    )(x, x, x, g_mix, w_in, conv_w, conv_b, gq, gk, cos_t, sa_t, sb_t, seg)


def _attn_finish(o1, l1, o2, l2, lq1_ref, lk1_ref, lq2_ref, lk2_ref, gs_ref, o_ref):
    lam = (jnp.exp(jnp.sum(lq1_ref[...] * lk1_ref[...], keepdims=True))
           - jnp.exp(jnp.sum(lq2_ref[...] * lk2_ref[...], keepdims=True)) + LAM_INIT)
    o = o1 / l1 - lam * (o2 / l2)
    o_ref[0] = (_rms(o, gs_ref[...]) * (1.0 - LAM_INIT)).astype(BF16)


def _diff_attn_online_kernel(q_ref, k1_ref, k2_ref, v_ref, lq1_ref, lk1_ref, lq2_ref, lk2_ref,
                             gs_ref, o_ref, m_ref, l_ref, acc_ref, *, tk):
    S = k1_ref.shape[2]
    q = q_ref[0, 0]

    m_ref[...] = jnp.full(m_ref.shape, -jnp.inf, F32)
    l_ref[...] = jnp.zeros(l_ref.shape, F32)
    acc_ref[...] = jnp.zeros(acc_ref.shape, F32)

    def body(j, carry):
        ks = pl.ds(pl.multiple_of(j * tk, tk), tk)
        v = v_ref[0, 0, ks, :ATTN_V_DIM]
        for c, k_ref in enumerate((k1_ref, k2_ref)):
            s = _dot_t(q, k_ref[0, 0, ks, :])
            m_prev = m_ref[c]
            m_new = jnp.maximum(m_prev, jnp.max(s, axis=-1, keepdims=True))
            alpha = jnp.exp(m_prev - m_new)
            p = jnp.exp(s - m_new)
            l_ref[c] = alpha * l_ref[c] + jnp.sum(p, axis=-1, keepdims=True)
            acc_ref[c] = alpha * acc_ref[c] + _dot(p.astype(BF16), v)
            m_ref[c] = m_new
        return carry

    lax.fori_loop(0, S // tk, body, 0)
    _attn_finish(acc_ref[0], l_ref[0], acc_ref[1], l_ref[1],
                 lq1_ref, lk1_ref, lq2_ref, lk2_ref, gs_ref, o_ref)


def _diff_attn_bounded_kernel(q_ref, k1_ref, k2_ref, v_ref, lq1_ref, lk1_ref, lq2_ref, lk2_ref,
                              gs_ref, o_ref, acc_ref, *, tk):
    S = k1_ref.shape[2]
    q = q_ref[0, 0]
    acc_ref[...] = jnp.zeros(acc_ref.shape, F32)

    def body(j, carry):
        ks = pl.ds(pl.multiple_of(j * tk, tk), tk)
        v1 = v_ref[0, 0, ks, :]
        for c, k_ref in enumerate((k1_ref, k2_ref)):
            s = _dot_t(q, k_ref[0, 0, ks, :])
            acc_ref[c] += _dot(jnp.exp(s).astype(BF16), v1)
        return carry

    lax.fori_loop(0, S // tk, body, 0)
    _attn_finish(acc_ref[0, :, :ATTN_V_DIM], acc_ref[0, :, ATTN_V_DIM:],
                 acc_ref[1, :, :ATTN_V_DIM], acc_ref[1, :, ATTN_V_DIM:],
                 lq1_ref, lk1_ref, lq2_ref, lk2_ref, gs_ref, o_ref)


def _diff_attn(q, k1, k2, v1, lq1, lk1, lq2, lk2, g_subln, *, bounded, tq, tk):
    B, H, S, E = q.shape
    kv_spec = pl.BlockSpec((1, 1, S, E), lambda b, h, i: (b, h, 0, 0))
    v1_spec = pl.BlockSpec((1, 1, S, 2 * E), lambda b, h, i: (b, h, 0, 0))
    lam_spec = _resident((1, ATTN_HEAD_DIM))
    if bounded:
        body = functools.partial(_diff_attn_bounded_kernel, tk=tk)
        scratch = [pltpu.VMEM((2, tq, 2 * E), F32)]
    else:
        body = functools.partial(_diff_attn_online_kernel, tk=tk)
        scratch = [pltpu.VMEM((2, tq, 1), F32), pltpu.VMEM((2, tq, 1), F32),
                   pltpu.VMEM((2, tq, E), F32)]
    return pl.pallas_call(
        body,
        grid=(B, H, S // tq),
        in_specs=[
            pl.BlockSpec((1, 1, tq, E), lambda b, h, i: (b, h, i, 0)),
            kv_spec, kv_spec, v1_spec,
            lam_spec, lam_spec, lam_spec, lam_spec,
            _resident((1, E)),
        ],
        out_specs=pl.BlockSpec((1, tq, E), lambda b, h, i: (b, i, h)),
        out_shape=jax.ShapeDtypeStruct((B, S, H * E), BF16),
        scratch_shapes=scratch,
        compiler_params=pltpu.CompilerParams(
            dimension_semantics=("arbitrary", "arbitrary", "arbitrary"),
            vmem_limit_bytes=VMEM_LIMIT_BYTES),
        name="diff_attn_bounded" if bounded else "diff_attn_online",
    )(q, k1, k2, v1, lq1, lk1, lq2, lk2, g_subln)


def _tail_kernel(x_ref, yc_ref, ya_ref, wo_ref, gmq_ref, wq_ref, gqm_ref, km_ref, vm_ref, wmo_ref,
                 gmlp_ref, w1_ref, w2_ref, o_ref):
    x = (x_ref[0] + _dot(yc_ref[0], wo_ref[:CONV_WIDTH, :])
         + _dot(ya_ref[0], wo_ref[CONV_WIDTH:, :]))

    hq = _dot(_rms(x, gmq_ref[...]).astype(BF16), wq_ref[...])
    heads = []
    for h in range(MEM_HEADS):
        qh = hq[:, h * MEM_HEAD_DIM:(h + 1) * MEM_HEAD_DIM]
        qh = (_rms(qh, gqm_ref[...]) * (MEM_HEAD_DIM ** -0.5)).astype(BF16)
        s = _dot_t(qh, km_ref[0, h])
        p = jnp.exp(s - jnp.max(s, axis=-1, keepdims=True))
        oh = _dot(p.astype(BF16), vm_ref[0, h]) / jnp.sum(p, axis=-1, keepdims=True)
        heads.append(oh.astype(BF16))
    x = x + _dot(jnp.concatenate(heads, axis=-1), wmo_ref[...])

    hn = _rms(x, gmlp_ref[...]).astype(BF16)
    y = x
    for c in range(D_FF // FF_CHUNK):
        cs = slice(c * FF_CHUNK, (c + 1) * FF_CHUNK)
        hf = jnp.maximum(_dot(hn, w1_ref[:, cs]), 0.0)
        y = y + _dot((hf * hf).astype(BF16), w2_ref[cs, :])
    o_ref[0] = y


def _tail(x, y_conv, y_attn, w_out, g_memq, wm_q, q_norm_mem, k_mem, v_mem, wm_o, g_mlp,
          w_ff1, w_ff2, *, tm):
    B, S, D = x.shape
    M = k_mem.shape[2]
    tok = lambda w: pl.BlockSpec((1, tm, w), lambda b, i: (b, i, 0))
    mem_spec = pl.BlockSpec((1, MEM_HEADS, M, MEM_HEAD_DIM), lambda b, i: (b, 0, 0, 0))
    return pl.pallas_call(
        _tail_kernel,
        grid=(B, S // tm),
        in_specs=[
            tok(D), tok(CONV_WIDTH), tok(ATTN_WIDTH),
            _resident(w_out.shape),
            _resident((1, D)),
            _resident(wm_q.shape),
            _resident((1, MEM_HEAD_DIM)),
            mem_spec, mem_spec,
            _resident(wm_o.shape),
            _resident((1, D)),
            _resident(w_ff1.shape),
            _resident(w_ff2.shape),
        ],
        out_specs=tok(D),
        out_shape=jax.ShapeDtypeStruct((B, S, D), F32),
        compiler_params=pltpu.CompilerParams(
            dimension_semantics=("arbitrary", "arbitrary"), vmem_limit_bytes=VMEM_LIMIT_BYTES),
        name="tail",
    )(x, y_conv, y_attn, w_out, g_memq, wm_q, q_norm_mem, k_mem, v_mem, wm_o, g_mlp, w_ff1, w_ff2)


def _rope_tables(S):
    pos = jnp.arange(S, dtype=F32)
    inv_freq = ROPE_THETA ** (-jnp.arange(0, ROPE_DIM, 2, dtype=F32) / ROPE_DIM)
    ang = pos[:, None] * inv_freq[None, :]
    cos, sin = jnp.cos(ang), jnp.sin(ang)
    one = jnp.ones((S, ATTN_HEAD_DIM - ROPE_DIM), F32)
    zero = jnp.zeros((S, ATTN_HEAD_DIM - ROPE_DIM), F32)
    zh = jnp.zeros((S, ROPE_HALF), F32)
    cos_c = jnp.concatenate([cos, cos, one], axis=-1)
    sa_c = jnp.concatenate([-sin, zh, zero], axis=-1)
    sb_c = jnp.concatenate([zh, sin, zero], axis=-1)
    tile2 = lambda t: jnp.concatenate([t, t], axis=-1)
    return tile2(cos_c), tile2(sa_c), tile2(sb_c)


def _segment_mean_matrix():
    lane = jnp.arange(LANES)
    same = (lane[:, None] // ATTN_HEAD_DIM) == (lane[None, :] // ATTN_HEAD_DIM)
    return jnp.where(same, 1.0 / ATTN_HEAD_DIM, 0.0).astype(BF16)


def _trunk(x, mem, p, *, tm, tq, tk, tq_bounded, tk_bounded):
    B, S, _ = x.shape
    k_mem, v_mem = _mem_kv(mem, p["g_memkv"], p["wm_kv"], p["k_norm_mem"])
    cos_t, sa_t, sb_t = _rope_tables(S)
    y_conv, q, k1, k2, v = _proj_in(
        x, p["g_mix"], p["w_in"], p["conv_w"], p["conv_b"], p["gq"], p["gk"],
        cos_t, sa_t, sb_t, p["seg"], tm=tm)
    attn = functools.partial(_diff_attn, q, k1, k2, v, p["lq1"], p["lk1"], p["lq2"], p["lk2"],
                             p["g_subln"])
    y_attn = lax.cond(p["score_bound"] <= SCORE_BOUND_MAX,
                      lambda: attn(bounded=True, tq=tq_bounded, tk=tk_bounded),
                      lambda: attn(bounded=False, tq=tq, tk=tk))
    return _tail(x, y_conv, y_attn, p["w_out"], p["g_memq"], p["wm_q"], p["q_norm_mem"],
                 k_mem, v_mem, p["wm_o"], p["g_mlp"], p["w_ff1"], p["w_ff2"], tm=tm)


def kernel(x_prompt, x_sample, mem_prompt, mem_sample, g_mix, w_in, conv_w, conv_b, q_norm, k_norm, lambda_q1, lambda_k1, lambda_q2, lambda_k2, g_subln, w_out, g_memq, g_memkv, wm_q, wm_kv, q_norm_mem, k_norm_mem, wm_o, g_mlp, w_ff1, w_ff2):
    l = LAYER
    tile2 = lambda t: jnp.concatenate([t, t], axis=-1)
    p = {
        "g_mix": g_mix[l][None], "w_in": w_in[l].astype(BF16),
        "conv_w": conv_w[l], "conv_b": conv_b[l][None],
        "gq": tile2(q_norm[l])[None] * (ATTN_HEAD_DIM ** -0.5), "gk": tile2(k_norm[l])[None],
        "seg": _segment_mean_matrix(),
        "score_bound": (1.02 * ATTN_HEAD_DIM ** 0.5 * jnp.max(jnp.abs(q_norm[l]))
                        * jnp.max(jnp.abs(k_norm[l]))),
        "lq1": lambda_q1[l][None], "lk1": lambda_k1[l][None],
        "lq2": lambda_q2[l][None], "lk2": lambda_k2[l][None],
        "g_subln": g_subln[l][None], "w_out": w_out[l].astype(BF16),
        "g_memq": g_memq[l][None], "g_memkv": g_memkv[l][None],
        "wm_q": wm_q[l].astype(BF16), "wm_kv": wm_kv[l].astype(BF16),
        "q_norm_mem": q_norm_mem[l][None], "k_norm_mem": k_norm_mem[l][None],
        "wm_o": wm_o[l].astype(BF16), "g_mlp": g_mlp[l][None],
        "w_ff1": w_ff1[l].astype(BF16), "w_ff2": w_ff2[l].astype(BF16),
    }
    tiles = dict(tm=512, tq=512, tk=512, tq_bounded=1024, tk_bounded=4096)
    y_prompt = _trunk(x_prompt, mem_prompt, p, **tiles)
    y_sample = _trunk(x_sample, mem_sample, p, **tiles)
    return (y_prompt, y_sample)
```

```python
import functools
import math

import jax
import jax.numpy as jnp
from jax import lax
from jax.experimental import pallas as pl
from jax.experimental.pallas import tpu as pltpu

LANES = 128
SUBLANES = 8
MXU_DIM = 256
VMEM_LIMIT_BYTES = 56 * 1024 * 1024

D_MODEL = 1024
CONV_WIDTH = D_MODEL // 2
ATTN_HEADS = 4
ATTN_HEAD_DIM = 64
ATTN_V_DIM = 2 * ATTN_HEAD_DIM
ATTN_WIDTH = ATTN_HEADS * ATTN_V_DIM
QK_WIDTH = ATTN_HEADS * 2 * ATTN_HEAD_DIM
ROPE_DIM = ATTN_HEAD_DIM // 4
ROPE_HALF = ROPE_DIM // 2
ROPE_THETA = 500000.0
MEM_HEADS = 4
MEM_HEAD_DIM = 128
MEM_WIDTH = MEM_HEADS * MEM_HEAD_DIM
D_FF = 4 * D_MODEL
FF_CHUNK = 1024
SEG_LANES = MXU_DIM
EPS = 1e-6
LAYER = 0
LAM_INIT = 0.8 - 0.6 * math.exp(-0.3 * LAYER)
SCORE_BOUND_MAX = 60.0

COL_B, COL_C, COL_V = 0, CONV_WIDTH, 2 * CONV_WIDTH
COL_Q = 3 * CONV_WIDTH
COL_K = COL_Q + QK_WIDTH
COL_AV = COL_K + QK_WIDTH

BF16 = jnp.bfloat16
F32 = jnp.float32


def _resident(shape):
    nd = len(shape)
    return pl.BlockSpec(shape, lambda *_: (0,) * nd, pipeline_mode=pl.Buffered(1))


def _rms(x, g):
    return x * lax.rsqrt(jnp.mean(x * x, axis=-1, keepdims=True) + EPS) * g


def _dot(a, b):
    return jnp.dot(a, b, preferred_element_type=F32)


def _dot_t(a, b):
    return lax.dot_general(a, b, (((1,), (1,)), ((), ())), preferred_element_type=F32)


def _mem_kv_kernel(mem_ref, g_ref, w_ref, gk_ref, k_ref, v_ref):
    m = _rms(mem_ref[0], g_ref[...]).astype(BF16)
    kv = _dot(m, w_ref[...])
    for h in range(MEM_HEADS):
        kh = kv[:, h * MEM_HEAD_DIM:(h + 1) * MEM_HEAD_DIM]
        k_ref[0, h] = _rms(kh, gk_ref[...]).astype(BF16)
        v_ref[0, h] = kv[:, MEM_WIDTH + h * MEM_HEAD_DIM:
                         MEM_WIDTH + (h + 1) * MEM_HEAD_DIM].astype(BF16)


def _mem_kv(mem, g_memkv, wm_kv, k_norm_mem):
    B, M, D = mem.shape
    out = jax.ShapeDtypeStruct((B, MEM_HEADS, M, MEM_HEAD_DIM), BF16)
    head_spec = pl.BlockSpec((1, MEM_HEADS, M, MEM_HEAD_DIM), lambda b: (b, 0, 0, 0))
    return pl.pallas_call(
        _mem_kv_kernel,
        grid=(B,),
        in_specs=[
            pl.BlockSpec((1, M, D), lambda b: (b, 0, 0)),
            _resident((1, D)),
            _resident((D, 2 * MEM_WIDTH)),
            _resident((1, MEM_HEAD_DIM)),
        ],
        out_specs=[head_spec, head_spec],
        out_shape=[out, out],
        compiler_params=pltpu.CompilerParams(
            dimension_semantics=("arbitrary",), vmem_limit_bytes=VMEM_LIMIT_BYTES),
        name="mem_kv",
    )(mem, g_memkv, wm_kv, k_norm_mem)


def _proj_in_kernel(x_ref, xp_ref, xn_ref, g_ref, w_ref, cw_ref, cb_ref, gq_ref, gk_ref,
                    cos_ref, sa_ref, sb_ref, seg_ref,
                    yc_ref, q_ref, k1_ref, k2_ref, v_ref, *, tm, sub):
    i = pl.program_id(1)
    n_tiles = pl.num_programs(1)
    n_sub = tm // sub
    g = g_ref[...]
    seg = seg_ref[...]
    lane = lax.broadcasted_iota(jnp.int32, (sub, LANES), 1)
    first = lane < ATTN_HEAD_DIM
    halo = jnp.concatenate([xp_ref[0], xn_ref[0]], axis=0)
    hh = _rms(halo, g).astype(BF16)

    def norm_rope(t, gain, cos, sa, sb):
        outs = []
        for c in range(QK_WIDTH // SEG_LANES):
            tc = t[:, c * SEG_LANES:(c + 1) * SEG_LANES]
            ms = _dot((tc * tc).astype(BF16), seg)
            tc = tc * lax.rsqrt(ms + EPS)
            for hd in range(SEG_LANES // LANES):
                th = tc[:, hd * LANES:(hd + 1) * LANES] * gain
                outs.append(th * cos + pltpu.roll(th, LANES - ROPE_HALF, axis=1) * sa
                            + pltpu.roll(th, ROPE_HALF, axis=1) * sb)
        return outs

    ub, z = [], []
    zh = None
    for r in range(n_sub):
        rows = slice(r * sub, (r + 1) * sub)
        h = _rms(x_ref[0, rows, :], g).astype(BF16)
        cos, sa, sb = cos_ref[rows, :], sa_ref[rows, :], sb_ref[rows, :]

        qs = norm_rope(_dot(h, w_ref[:, COL_Q:COL_K]), gq_ref[...], cos, sa, sb)
        for hd in range(ATTN_HEADS):
            q_ref[0, hd, rows, :] = qs[hd].astype(BF16)
        ks = norm_rope(_dot(h, w_ref[:, COL_K:COL_AV]), gk_ref[...], cos, sa, sb)
        for hd in range(ATTN_HEADS):
            k1_ref[0, hd, rows, :] = jnp.where(first, ks[hd], 0.0).astype(BF16)
            k2_ref[0, hd, rows, :] = jnp.where(first, 0.0, ks[hd]).astype(BF16)

        lhs = jnp.concatenate([h, hh], axis=0) if r == 0 else h
        ucv = _dot(lhs, w_ref[:, COL_B:COL_Q])
        if r == 0:
            zh = ucv[sub:, COL_C:COL_V] * ucv[sub:, COL_V:COL_Q]
        ub.append(ucv[:sub, COL_B:COL_C])
        z.append(ucv[:sub, COL_C:COL_V] * ucv[:sub, COL_V:COL_Q])

        uv = _dot(h, w_ref[:, COL_AV:])
        for hd in range(ATTN_HEADS):
            v_ref[0, hd, rows, :ATTN_V_DIM] = uv[:, hd * ATTN_V_DIM:(hd + 1) * ATTN_V_DIM].astype(BF16)
            v_ref[0, hd, rows, ATTN_V_DIM:] = jnp.ones((sub, ATTN_V_DIM), BF16)

    z_before = jnp.where(i > 0, zh[SUBLANES - 1:SUBLANES], 0.0)
    z_after = jnp.where(i < n_tiles - 1, zh[SUBLANES:SUBLANES + 1], 0.0)
    row = lax.broadcasted_iota(jnp.int32, (sub, CONV_WIDTH), 0)
    for r in range(n_sub):
        before = z_before if r == 0 else z[r - 1][sub - 1:sub]
        after = z_after if r == n_sub - 1 else z[r + 1][0:1]
        z_prev = jnp.where(row == 0, before, pltpu.roll(z[r], 1, axis=0))
        z_next = jnp.where(row == sub - 1, after, pltpu.roll(z[r], sub - 1, axis=0))
        conv = (z_prev * cw_ref[0:1, :] + z[r] * cw_ref[1:2, :] + z_next * cw_ref[2:3, :]
                + cb_ref[...])
        yc_ref[0, r * sub:(r + 1) * sub, :] = (ub[r] * conv).astype(BF16)


def _proj_in(x, g_mix, w_in, conv_w, conv_b, gq, gk, cos_t, sa_t, sb_t, seg, *, tm, sub):
    B, S, D = x.shape
    assert S % tm == 0 and tm % sub == 0, (S, tm, sub)
    n_tiles = S // tm
    rb = tm // SUBLANES
    n_rb = S // SUBLANES
    head_out = jax.ShapeDtypeStruct((B, ATTN_HEADS, S, ATTN_V_DIM), BF16)
    head_spec = pl.BlockSpec((1, ATTN_HEADS, tm, ATTN_V_DIM), lambda b, i: (b, 0, i, 0))
    v1_out = jax.ShapeDtypeStruct((B, ATTN_HEADS, S, 2 * ATTN_V_DIM), BF16)
    v1_spec = pl.BlockSpec((1, ATTN_HEADS, tm, 2 * ATTN_V_DIM), lambda b, i: (b, 0, i, 0))
    tab_spec = pl.BlockSpec((tm, LANES), lambda b, i: (i, 0))
    return pl.pallas_call(
        functools.partial(_proj_in_kernel, tm=tm, sub=sub),
        grid=(B, n_tiles),
        in_specs=[
            pl.BlockSpec((1, tm, D), lambda b, i: (b, i, 0)),
            pl.BlockSpec((1, SUBLANES, D), lambda b, i: (b, jnp.maximum(i * rb - 1, 0), 0)),
            pl.BlockSpec((1, SUBLANES, D), lambda b, i: (b, jnp.minimum((i + 1) * rb, n_rb - 1), 0)),
            _resident((1, D)),
            _resident(w_in.shape),
            _resident(conv_w.shape),
            _resident((1, CONV_WIDTH)),
            _resident((1, LANES)),
            _resident((1, LANES)),
            tab_spec, tab_spec, tab_spec,
            _resident((SEG_LANES, SEG_LANES)),
        ],
        out_specs=[
            pl.BlockSpec((1, tm, CONV_WIDTH), lambda b, i: (b, i, 0)),
            head_spec, head_spec, head_spec, v1_spec,
        ],
        out_shape=[
            jax.ShapeDtypeStruct((B, S, CONV_WIDTH), BF16),
            head_out, head_out, head_out, v1_out,
        ],
        compiler_params=pltpu.CompilerParams(
            dimension_semantics=("arbitrary", "arbitrary"), vmem_limit_bytes=VMEM_LIMIT_BYTES),
        name="proj_in",
    )(x, x, x, g_mix, w_in, conv_w, conv_b, gq, gk, cos_t, sa_t, sb_t, seg)


def _attn_finish(o1, l1, o2, l2, lq1_ref, lk1_ref, lq2_ref, lk2_ref, gs_ref, o_ref):
    lam = (jnp.exp(jnp.sum(lq1_ref[...] * lk1_ref[...], keepdims=True))
           - jnp.exp(jnp.sum(lq2_ref[...] * lk2_ref[...], keepdims=True)) + LAM_INIT)
    o = o1 / l1 - lam * (o2 / l2)
    o_ref[0] = (_rms(o, gs_ref[...]) * (1.0 - LAM_INIT)).astype(BF16)


def _diff_attn_online_kernel(q_ref, k1_ref, k2_ref, v_ref, lq1_ref, lk1_ref, lq2_ref, lk2_ref,
                             gs_ref, o_ref, m_ref, l_ref, acc_ref, *, tk):
    S = k1_ref.shape[2]
    q = q_ref[0, 0]

    m_ref[...] = jnp.full(m_ref.shape, -jnp.inf, F32)
    l_ref[...] = jnp.zeros(l_ref.shape, F32)
    acc_ref[...] = jnp.zeros(acc_ref.shape, F32)

    def body(j, carry):
        ks = pl.ds(pl.multiple_of(j * tk, tk), tk)
        v = v_ref[0, 0, ks, :ATTN_V_DIM]
        for c, k_ref in enumerate((k1_ref, k2_ref)):
            s = _dot_t(q, k_ref[0, 0, ks, :])
            m_prev = m_ref[c]
            m_new = jnp.maximum(m_prev, jnp.max(s, axis=-1, keepdims=True))
            alpha = jnp.exp(m_prev - m_new)
            p = jnp.exp(s - m_new)
            l_ref[c] = alpha * l_ref[c] + jnp.sum(p, axis=-1, keepdims=True)
            acc_ref[c] = alpha * acc_ref[c] + _dot(p.astype(BF16), v)
            m_ref[c] = m_new
        return carry

    lax.fori_loop(0, S // tk, body, 0)
    _attn_finish(acc_ref[0], l_ref[0], acc_ref[1], l_ref[1],
                 lq1_ref, lk1_ref, lq2_ref, lk2_ref, gs_ref, o_ref)


def _diff_attn_bounded_kernel(q_ref, k1_ref, k2_ref, v_ref, lq1_ref, lk1_ref, lq2_ref, lk2_ref,
                              gs_ref, o_ref, acc_ref, *, tk):
    S = k1_ref.shape[2]
    q = q_ref[0, 0]
    acc_ref[...] = jnp.zeros(acc_ref.shape, F32)

    def body(j, carry):
        ks = pl.ds(pl.multiple_of(j * tk, tk), tk)
        v1 = v_ref[0, 0, ks, :]
        for c, k_ref in enumerate((k1_ref, k2_ref)):
            s = _dot_t(q, k_ref[0, 0, ks, :])
            acc_ref[c] += _dot(jnp.exp(s).astype(BF16), v1)
        return carry

    lax.fori_loop(0, S // tk, body, 0)
    _attn_finish(acc_ref[0, :, :ATTN_V_DIM], acc_ref[0, :, ATTN_V_DIM:],
                 acc_ref[1, :, :ATTN_V_DIM], acc_ref[1, :, ATTN_V_DIM:],
                 lq1_ref, lk1_ref, lq2_ref, lk2_ref, gs_ref, o_ref)


def _diff_attn(q, k1, k2, v1, lq1, lk1, lq2, lk2, g_subln, *, bounded, tq, tk):
    B, H, S, E = q.shape
    tq, tk = min(tq, S), min(tk, S)
    assert S % tq == 0 and S % tk == 0, (S, tq, tk)
    kv_spec = pl.BlockSpec((1, 1, S, E), lambda b, h, i: (b, h, 0, 0))
    v1_spec = pl.BlockSpec((1, 1, S, 2 * E), lambda b, h, i: (b, h, 0, 0))
    lam_spec = _resident((1, ATTN_HEAD_DIM))
    if bounded:
        body = functools.partial(_diff_attn_bounded_kernel, tk=tk)
        scratch = [pltpu.VMEM((2, tq, 2 * E), F32)]
    else:
        body = functools.partial(_diff_attn_online_kernel, tk=tk)
        scratch = [pltpu.VMEM((2, tq, 1), F32), pltpu.VMEM((2, tq, 1), F32),
                   pltpu.VMEM((2, tq, E), F32)]
    return pl.pallas_call(
        body,
        grid=(B, H, S // tq),
        in_specs=[
            pl.BlockSpec((1, 1, tq, E), lambda b, h, i: (b, h, i, 0)),
            kv_spec, kv_spec, v1_spec,
            lam_spec, lam_spec, lam_spec, lam_spec,
            _resident((1, E)),
        ],
        out_specs=pl.BlockSpec((1, tq, E), lambda b, h, i: (b, i, h)),
        out_shape=jax.ShapeDtypeStruct((B, S, H * E), BF16),
        scratch_shapes=scratch,
        compiler_params=pltpu.CompilerParams(
            dimension_semantics=("arbitrary", "arbitrary", "arbitrary"),
            vmem_limit_bytes=VMEM_LIMIT_BYTES),
        name="diff_attn_bounded" if bounded else "diff_attn_online",
    )(q, k1, k2, v1, lq1, lk1, lq2, lk2, g_subln)


def _tail_kernel(x_ref, yc_ref, ya_ref, wo_ref, gmq_ref, wq_ref, gqm_ref, km_ref, vm_ref, wmo_ref,
                 gmlp_ref, w1_ref, w2_ref, o_ref):
    x = (x_ref[0] + _dot(yc_ref[0], wo_ref[:CONV_WIDTH, :])
         + _dot(ya_ref[0], wo_ref[CONV_WIDTH:, :]))

    hq = _dot(_rms(x, gmq_ref[...]).astype(BF16), wq_ref[...])
    heads = []
    for h in range(MEM_HEADS):
        qh = hq[:, h * MEM_HEAD_DIM:(h + 1) * MEM_HEAD_DIM]
        qh = (_rms(qh, gqm_ref[...]) * (MEM_HEAD_DIM ** -0.5)).astype(BF16)
        s = _dot_t(qh, km_ref[0, h])
        p = jnp.exp(s - jnp.max(s, axis=-1, keepdims=True))
        oh = _dot(p.astype(BF16), vm_ref[0, h]) / jnp.sum(p, axis=-1, keepdims=True)
        heads.append(oh.astype(BF16))
    x = x + _dot(jnp.concatenate(heads, axis=-1), wmo_ref[...])

    hn = _rms(x, gmlp_ref[...]).astype(BF16)
    y = x
    for c in range(D_FF // FF_CHUNK):
        cs = slice(c * FF_CHUNK, (c + 1) * FF_CHUNK)
        hf = jnp.maximum(_dot(hn, w1_ref[:, cs]), 0.0)
        y = y + _dot((hf * hf).astype(BF16), w2_ref[cs, :])
    o_ref[0] = y


def _tail(x, y_conv, y_attn, w_out, g_memq, wm_q, q_norm_mem, k_mem, v_mem, wm_o, g_mlp,
          w_ff1, w_ff2, *, tm):
    B, S, D = x.shape
    M = k_mem.shape[2]
    tok = lambda w: pl.BlockSpec((1, tm, w), lambda b, i: (b, i, 0))
    mem_spec = pl.BlockSpec((1, MEM_HEADS, M, MEM_HEAD_DIM), lambda b, i: (b, 0, 0, 0))
    return pl.pallas_call(
        _tail_kernel,
        grid=(B, S // tm),
        in_specs=[
            tok(D), tok(CONV_WIDTH), tok(ATTN_WIDTH),
            _resident(w_out.shape),
            _resident((1, D)),
            _resident(wm_q.shape),
            _resident((1, MEM_HEAD_DIM)),
            mem_spec, mem_spec,
            _resident(wm_o.shape),
            _resident((1, D)),
            _resident(w_ff1.shape),
            _resident(w_ff2.shape),
        ],
        out_specs=tok(D),
        out_shape=jax.ShapeDtypeStruct((B, S, D), F32),
        compiler_params=pltpu.CompilerParams(
            dimension_semantics=("arbitrary", "arbitrary"), vmem_limit_bytes=VMEM_LIMIT_BYTES),
        name="tail",
    )(x, y_conv, y_attn, w_out, g_memq, wm_q, q_norm_mem, k_mem, v_mem, wm_o, g_mlp, w_ff1, w_ff2)


def _rope_tables(S):
    pos = jnp.arange(S, dtype=F32)
    inv_freq = ROPE_THETA ** (-jnp.arange(0, ROPE_DIM, 2, dtype=F32) / ROPE_DIM)
    d = jnp.arange(LANES) % ATTN_HEAD_DIM
    ang = pos[:, None] * inv_freq[d % ROPE_HALF][None, :]
    lo = (d < ROPE_HALF)[None, :]
    hi = ((d >= ROPE_HALF) & (d < ROPE_DIM))[None, :]
    cos_t = jnp.where(lo | hi, jnp.cos(ang), 1.0)
    sa_t = jnp.where(lo, -jnp.sin(ang), 0.0)
    sb_t = jnp.where(hi, jnp.sin(ang), 0.0)
    return cos_t, sa_t, sb_t


def _segment_mean_matrix():
    lane = jnp.arange(SEG_LANES)
    same = (lane[:, None] // ATTN_HEAD_DIM) == (lane[None, :] // ATTN_HEAD_DIM)
    return jnp.where(same, 1.0 / ATTN_HEAD_DIM, 0.0).astype(BF16)


def _trunk(x, mem, p, *, tm_in, sub_in, tm, tq, tk, tq_bounded, tk_bounded):
    B, S, _ = x.shape
    k_mem, v_mem = _mem_kv(mem, p["g_memkv"], p["wm_kv"], p["k_norm_mem"])
    cos_t, sa_t, sb_t = _rope_tables(S)
    y_conv, q, k1, k2, v = _proj_in(
        x, p["g_mix"], p["w_in"], p["conv_w"], p["conv_b"], p["gq"], p["gk"],
        cos_t, sa_t, sb_t, p["seg"], tm=tm_in, sub=sub_in)
    attn = functools.partial(_diff_attn, q, k1, k2, v, p["lq1"], p["lk1"], p["lq2"], p["lk2"],
                             p["g_subln"])
    y_attn = lax.cond(p["score_bound"] <= SCORE_BOUND_MAX,
                      lambda: attn(bounded=True, tq=tq_bounded, tk=tk_bounded),
                      lambda: attn(bounded=False, tq=tq, tk=tk))
    return _tail(x, y_conv, y_attn, p["w_out"], p["g_memq"], p["wm_q"], p["q_norm_mem"],
                 k_mem, v_mem, p["wm_o"], p["g_mlp"], p["w_ff1"], p["w_ff2"], tm=tm)


def kernel(x_prompt, x_sample, mem_prompt, mem_sample, g_mix, w_in, conv_w, conv_b, q_norm, k_norm, lambda_q1, lambda_k1, lambda_q2, lambda_k2, g_subln, w_out, g_memq, g_memkv, wm_q, wm_kv, q_norm_mem, k_norm_mem, wm_o, g_mlp, w_ff1, w_ff2):
    l = LAYER
    tile2 = lambda t: jnp.concatenate([t, t], axis=-1)
    p = {
        "g_mix": g_mix[l][None], "w_in": w_in[l].astype(BF16),
        "conv_w": conv_w[l], "conv_b": conv_b[l][None],
        "gq": tile2(q_norm[l])[None] * (ATTN_HEAD_DIM ** -0.5), "gk": tile2(k_norm[l])[None],
        "seg": _segment_mean_matrix(),
        "score_bound": (1.02 * ATTN_HEAD_DIM ** 0.5 * jnp.max(jnp.abs(q_norm[l]))
                        * jnp.max(jnp.abs(k_norm[l]))),
        "lq1": lambda_q1[l][None], "lk1": lambda_k1[l][None],
        "lq2": lambda_q2[l][None], "lk2": lambda_k2[l][None],
        "g_subln": g_subln[l][None], "w_out": w_out[l].astype(BF16),
        "g_memq": g_memq[l][None], "g_memkv": g_memkv[l][None],
        "wm_q": wm_q[l].astype(BF16), "wm_kv": wm_kv[l].astype(BF16),
        "q_norm_mem": q_norm_mem[l][None], "k_norm_mem": k_norm_mem[l][None],
        "wm_o": wm_o[l].astype(BF16), "g_mlp": g_mlp[l][None],
        "w_ff1": w_ff1[l].astype(BF16), "w_ff2": w_ff2[l].astype(BF16),
    }
    tiles = dict(tm_in=1024, sub_in=512, tm=512, tq=512, tk=512, tq_bounded=1024, tk_bounded=4096)
    y_prompt = _trunk(x_prompt, mem_prompt, p, **tiles)
    y_sample = _trunk(x_sample, mem_sample, p, **tiles)
    return (y_prompt, y_sample)
```

```python
import functools
import math

import jax
import jax.numpy as jnp
from jax import lax
from jax.experimental import pallas as pl
from jax.experimental.pallas import tpu as pltpu

LANES = 128
SUBLANES = 8
MXU_DIM = 256
VMEM_LIMIT_BYTES = 56 * 1024 * 1024

D_MODEL = 1024
CONV_WIDTH = D_MODEL // 2
ATTN_HEADS = 4
ATTN_HEAD_DIM = 64
ATTN_V_DIM = 2 * ATTN_HEAD_DIM
ATTN_WIDTH = ATTN_HEADS * ATTN_V_DIM
QK_WIDTH = ATTN_HEADS * 2 * ATTN_HEAD_DIM
ROPE_DIM = ATTN_HEAD_DIM // 4
ROPE_HALF = ROPE_DIM // 2
ROPE_THETA = 500000.0
MEM_HEADS = 4
MEM_HEAD_DIM = 128
MEM_WIDTH = MEM_HEADS * MEM_HEAD_DIM
D_FF = 4 * D_MODEL
FF_CHUNK = 1024
VT_ROWS = ATTN_V_DIM + 16
SEG_LANES = MXU_DIM
EPS = 1e-6
LAYER = 0
LAM_INIT = 0.8 - 0.6 * math.exp(-0.3 * LAYER)
SCORE_BOUND_MAX = 60.0

COL_B, COL_C, COL_V = 0, CONV_WIDTH, 2 * CONV_WIDTH
COL_Q = 3 * CONV_WIDTH
COL_K = COL_Q + QK_WIDTH
COL_AV = COL_K + QK_WIDTH

BF16 = jnp.bfloat16
F32 = jnp.float32


def _resident(shape):
    nd = len(shape)
    return pl.BlockSpec(shape, lambda *_: (0,) * nd, pipeline_mode=pl.Buffered(1))


def _rms(x, g):
    return x * lax.rsqrt(jnp.mean(x * x, axis=-1, keepdims=True) + EPS) * g


def _dot(a, b):
    return jnp.dot(a, b, preferred_element_type=F32)


def _dot_t(a, b):
    return lax.dot_general(a, b, (((1,), (1,)), ((), ())), preferred_element_type=F32)


def _mem_kv_kernel(mem_ref, g_ref, w_ref, gk_ref, k_ref, v_ref):
    m = _rms(mem_ref[0], g_ref[...]).astype(BF16)
    kv = _dot(m, w_ref[...])
    for h in range(MEM_HEADS):
        kh = kv[:, h * MEM_HEAD_DIM:(h + 1) * MEM_HEAD_DIM]
        k_ref[0, h] = _rms(kh, gk_ref[...]).astype(BF16)
        v_ref[0, h] = kv[:, MEM_WIDTH + h * MEM_HEAD_DIM:
                         MEM_WIDTH + (h + 1) * MEM_HEAD_DIM].astype(BF16)


def _mem_kv(mem, g_memkv, wm_kv, k_norm_mem):
    B, M, D = mem.shape
    out = jax.ShapeDtypeStruct((B, MEM_HEADS, M, MEM_HEAD_DIM), BF16)
    head_spec = pl.BlockSpec((1, MEM_HEADS, M, MEM_HEAD_DIM), lambda b: (b, 0, 0, 0))
    return pl.pallas_call(
        _mem_kv_kernel,
        grid=(B,),
        in_specs=[
            pl.BlockSpec((1, M, D), lambda b: (b, 0, 0)),
            _resident((1, D)),
            _resident((D, 2 * MEM_WIDTH)),
            _resident((1, MEM_HEAD_DIM)),
        ],
        out_specs=[head_spec, head_spec],
        out_shape=[out, out],
        compiler_params=pltpu.CompilerParams(
            dimension_semantics=("arbitrary",), vmem_limit_bytes=VMEM_LIMIT_BYTES),
        name="mem_kv",
    )(mem, g_memkv, wm_kv, k_norm_mem)


def _proj_in_kernel(x_ref, xp_ref, xn_ref, g_ref, w_ref, cw_ref, cb_ref, gq_ref, gk_ref,
                    cos_ref, sa_ref, sb_ref, seg_ref,
                    yc_ref, q_ref, k1_ref, k2_ref, v_ref, *, tm, sub):
    i = pl.program_id(1)
    n_tiles = pl.num_programs(1)
    n_sub = tm // sub
    g = g_ref[...]
    seg = seg_ref[...]
    lane = lax.broadcasted_iota(jnp.int32, (sub, LANES), 1)
    first = lane < ATTN_HEAD_DIM
    halo = jnp.concatenate([xp_ref[0], xn_ref[0]], axis=0)
    hh = _rms(halo, g).astype(BF16)

    def norm_rope(t, gain, cos, sa, sb):
        outs = []
        for c in range(QK_WIDTH // SEG_LANES):
            tc = t[:, c * SEG_LANES:(c + 1) * SEG_LANES]
            ms = _dot((tc * tc).astype(BF16), seg)
            tc = tc * lax.rsqrt(ms + EPS)
            for hd in range(SEG_LANES // LANES):
                th = tc[:, hd * LANES:(hd + 1) * LANES] * gain
                outs.append(th * cos + pltpu.roll(th, LANES - ROPE_HALF, axis=1) * sa
                            + pltpu.roll(th, ROPE_HALF, axis=1) * sb)
        return outs

    ub, z = [], []
    zh = None
    for r in range(n_sub):
        rows = slice(r * sub, (r + 1) * sub)
        h = _rms(x_ref[0, rows, :], g).astype(BF16)
        cos, sa, sb = cos_ref[rows, :], sa_ref[rows, :], sb_ref[rows, :]

        qs = norm_rope(_dot(h, w_ref[:, COL_Q:COL_K]), gq_ref[...], cos, sa, sb)
        for hd in range(ATTN_HEADS):
            q_ref[0, hd, rows, :] = qs[hd].astype(BF16)
        ks = norm_rope(_dot(h, w_ref[:, COL_K:COL_AV]), gk_ref[...], cos, sa, sb)
        for hd in range(ATTN_HEADS):
            k1_ref[0, hd, rows, :] = jnp.where(first, ks[hd], 0.0).astype(BF16)
            k2_ref[0, hd, rows, :] = jnp.where(first, 0.0, ks[hd]).astype(BF16)

        lhs = jnp.concatenate([h, hh], axis=0) if r == 0 else h
        ucv = _dot(lhs, w_ref[:, COL_B:COL_Q])
        if r == 0:
            zh = ucv[sub:, COL_C:COL_V] * ucv[sub:, COL_V:COL_Q]
        ub.append(ucv[:sub, COL_B:COL_C])
        z.append(ucv[:sub, COL_C:COL_V] * ucv[:sub, COL_V:COL_Q])

        uv = _dot(h, w_ref[:, COL_AV:])
        for hd in range(ATTN_HEADS):
            v_ref[0, hd, :ATTN_V_DIM, rows] = uv[:, hd * ATTN_V_DIM:(hd + 1) * ATTN_V_DIM].T.astype(BF16)
            v_ref[0, hd, ATTN_V_DIM:, rows] = jnp.ones((VT_ROWS - ATTN_V_DIM, sub), BF16)

    z_before = jnp.where(i > 0, zh[SUBLANES - 1:SUBLANES], 0.0)
    z_after = jnp.where(i < n_tiles - 1, zh[SUBLANES:SUBLANES + 1], 0.0)
    row = lax.broadcasted_iota(jnp.int32, (sub, CONV_WIDTH), 0)
    for r in range(n_sub):
        before = z_before if r == 0 else z[r - 1][sub - 1:sub]
        after = z_after if r == n_sub - 1 else z[r + 1][0:1]
        z_prev = jnp.where(row == 0, before, pltpu.roll(z[r], 1, axis=0))
        z_next = jnp.where(row == sub - 1, after, pltpu.roll(z[r], sub - 1, axis=0))
        conv = (z_prev * cw_ref[0:1, :] + z[r] * cw_ref[1:2, :] + z_next * cw_ref[2:3, :]
                + cb_ref[...])
        yc_ref[0, r * sub:(r + 1) * sub, :] = (ub[r] * conv).astype(BF16)


def _proj_in(x, g_mix, w_in, conv_w, conv_b, gq, gk, cos_t, sa_t, sb_t, seg, *, tm, sub):
    B, S, D = x.shape
    assert S % tm == 0 and tm % sub == 0, (S, tm, sub)
    n_tiles = S // tm
    rb = tm // SUBLANES
    n_rb = S // SUBLANES
    head_out = jax.ShapeDtypeStruct((B, ATTN_HEADS, S, ATTN_V_DIM), BF16)
    head_spec = pl.BlockSpec((1, ATTN_HEADS, tm, ATTN_V_DIM), lambda b, i: (b, 0, i, 0))
    v1_out = jax.ShapeDtypeStruct((B, ATTN_HEADS, VT_ROWS, S), BF16)
    v1_spec = pl.BlockSpec((1, ATTN_HEADS, VT_ROWS, tm), lambda b, i: (b, 0, 0, i))
    tab_spec = pl.BlockSpec((tm, LANES), lambda b, i: (i, 0))
    return pl.pallas_call(
        functools.partial(_proj_in_kernel, tm=tm, sub=sub),
        grid=(B, n_tiles),
        in_specs=[
            pl.BlockSpec((1, tm, D), lambda b, i: (b, i, 0)),
            pl.BlockSpec((1, SUBLANES, D), lambda b, i: (b, jnp.maximum(i * rb - 1, 0), 0)),
            pl.BlockSpec((1, SUBLANES, D), lambda b, i: (b, jnp.minimum((i + 1) * rb, n_rb - 1), 0)),
            _resident((1, D)),
            _resident(w_in.shape),
            _resident(conv_w.shape),
            _resident((1, CONV_WIDTH)),
            _resident((1, LANES)),
            _resident((1, LANES)),
            tab_spec, tab_spec, tab_spec,
            _resident((SEG_LANES, SEG_LANES)),
        ],
        out_specs=[
            pl.BlockSpec((1, tm, CONV_WIDTH), lambda b, i: (b, i, 0)),
            head_spec, head_spec, head_spec, v1_spec,
        ],
        out_shape=[
            jax.ShapeDtypeStruct((B, S, CONV_WIDTH), BF16),
            head_out, head_out, head_out, v1_out,
        ],
        compiler_params=pltpu.CompilerParams(
            dimension_semantics=("arbitrary", "arbitrary"), vmem_limit_bytes=VMEM_LIMIT_BYTES),
        name="proj_in",
    )(x, x, x, g_mix, w_in, conv_w, conv_b, gq, gk, cos_t, sa_t, sb_t, seg)


def _attn_finish(o1, o2, lq1_ref, lk1_ref, lq2_ref, lk2_ref, gs_ref, o_ref):
    lam = (jnp.exp(jnp.sum(lq1_ref[...] * lk1_ref[...], keepdims=True))
           - jnp.exp(jnp.sum(lq2_ref[...] * lk2_ref[...], keepdims=True)) + LAM_INIT)
    o = o1 - lam * o2
    o_ref[0] = (_rms(o, gs_ref[...]) * (1.0 - LAM_INIT)).astype(BF16)


def _diff_attn_online_kernel(q_ref, k1_ref, k2_ref, v_ref, lq1_ref, lk1_ref, lq2_ref, lk2_ref,
                             gs_ref, o_ref, m_ref, l_ref, acc_ref, *, tk):
    S = k1_ref.shape[2]
    q = q_ref[0, 0]

    m_ref[...] = jnp.full(m_ref.shape, -jnp.inf, F32)
    l_ref[...] = jnp.zeros(l_ref.shape, F32)
    acc_ref[...] = jnp.zeros(acc_ref.shape, F32)

    def body(j, carry):
        ks = pl.ds(pl.multiple_of(j * tk, tk), tk)
        vt = v_ref[0, 0, :ATTN_V_DIM, ks]
        for c, k_ref in enumerate((k1_ref, k2_ref)):
            s = _dot_t(q, k_ref[0, 0, ks, :])
            m_prev = m_ref[c]
            m_new = jnp.maximum(m_prev, jnp.max(s, axis=-1, keepdims=True))
            alpha = jnp.exp(m_prev - m_new)
            p = jnp.exp(s - m_new)
            l_ref[c] = alpha * l_ref[c] + jnp.sum(p, axis=-1, keepdims=True)
            acc_ref[c] = alpha * acc_ref[c] + _dot_t(p.astype(BF16), vt)
            m_ref[c] = m_new
        return carry

    lax.fori_loop(0, S // tk, body, 0)
    _attn_finish(acc_ref[0] / l_ref[0], acc_ref[1] / l_ref[1],
                 lq1_ref, lk1_ref, lq2_ref, lk2_ref, gs_ref, o_ref)


def _diff_attn_bounded_kernel(q_ref, k1_ref, k2_ref, v_ref, lq1_ref, lk1_ref, lq2_ref, lk2_ref,
                              gs_ref, o_ref, acc_ref, *, tk):
    S = k1_ref.shape[2]
    q = q_ref[0, 0]
    acc_ref[...] = jnp.zeros(acc_ref.shape, F32)

    def body(j, carry):
        ks = pl.ds(pl.multiple_of(j * tk, tk), tk)
        vt1 = v_ref[0, 0, :, ks]
        for c, k_ref in enumerate((k1_ref, k2_ref)):
            st = _dot_t(k_ref[0, 0, ks, :], q)
            acc_ref[c] += _dot(vt1, jnp.exp(st).astype(BF16))
        return carry

    lax.fori_loop(0, S // tk, body, 0)
    ot = [acc_ref[c, :ATTN_V_DIM, :] / acc_ref[c, ATTN_V_DIM:ATTN_V_DIM + 1, :] for c in range(2)]
    _attn_finish(ot[0].T, ot[1].T, lq1_ref, lk1_ref, lq2_ref, lk2_ref, gs_ref, o_ref)


def _diff_attn(q, k1, k2, v1, lq1, lk1, lq2, lk2, g_subln, *, bounded, tq, tk):
    B, H, S, E = q.shape
    tq, tk = min(tq, S), min(tk, S)
    assert S % tq == 0 and S % tk == 0, (S, tq, tk)
    kv_spec = pl.BlockSpec((1, 1, S, E), lambda b, h, i: (b, h, 0, 0))
    v1_spec = pl.BlockSpec((1, 1, VT_ROWS, S), lambda b, h, i: (b, h, 0, 0))
    lam_spec = _resident((1, ATTN_HEAD_DIM))
    if bounded:
        body = functools.partial(_diff_attn_bounded_kernel, tk=tk)
        scratch = [pltpu.VMEM((2, VT_ROWS, tq), F32)]
    else:
        body = functools.partial(_diff_attn_online_kernel, tk=tk)
        scratch = [pltpu.VMEM((2, tq, 1), F32), pltpu.VMEM((2, tq, 1), F32),
                   pltpu.VMEM((2, tq, E), F32)]
    return pl.pallas_call(
        body,
        grid=(B, H, S // tq),
        in_specs=[
            pl.BlockSpec((1, 1, tq, E), lambda b, h, i: (b, h, i, 0)),
            kv_spec, kv_spec, v1_spec,
            lam_spec, lam_spec, lam_spec, lam_spec,
            _resident((1, E)),
        ],
        out_specs=pl.BlockSpec((1, tq, E), lambda b, h, i: (b, i, h)),
        out_shape=jax.ShapeDtypeStruct((B, S, H * E), BF16),
        scratch_shapes=scratch,
        compiler_params=pltpu.CompilerParams(
            dimension_semantics=("arbitrary", "arbitrary", "arbitrary"),
            vmem_limit_bytes=VMEM_LIMIT_BYTES),
        name="diff_attn_bounded" if bounded else "diff_attn_online",
    )(q, k1, k2, v1, lq1, lk1, lq2, lk2, g_subln)


def _tail_kernel(x_ref, yc_ref, ya_ref, wo_ref, gmq_ref, wq_ref, gqm_ref, km_ref, vm_ref, wmo_ref,
                 gmlp_ref, w1_ref, w2_ref, o_ref):
    x = (x_ref[0] + _dot(yc_ref[0], wo_ref[:CONV_WIDTH, :])
         + _dot(ya_ref[0], wo_ref[CONV_WIDTH:, :]))

    hq = _dot(_rms(x, gmq_ref[...]).astype(BF16), wq_ref[...])
    heads = []
    for h in range(MEM_HEADS):
        qh = hq[:, h * MEM_HEAD_DIM:(h + 1) * MEM_HEAD_DIM]
        qh = (_rms(qh, gqm_ref[...]) * (MEM_HEAD_DIM ** -0.5)).astype(BF16)
        s = _dot_t(qh, km_ref[0, h])
        p = jnp.exp(s - jnp.max(s, axis=-1, keepdims=True))
        oh = _dot(p.astype(BF16), vm_ref[0, h]) / jnp.sum(p, axis=-1, keepdims=True)
        heads.append(oh.astype(BF16))
    x = x + _dot(jnp.concatenate(heads, axis=-1), wmo_ref[...])

    hn = _rms(x, gmlp_ref[...]).astype(BF16)
    y = x
    for c in range(D_FF // FF_CHUNK):
        cs = slice(c * FF_CHUNK, (c + 1) * FF_CHUNK)
        hf = jnp.maximum(_dot(hn, w1_ref[:, cs]), 0.0)
        y = y + _dot((hf * hf).astype(BF16), w2_ref[cs, :])
    o_ref[0] = y


def _tail(x, y_conv, y_attn, w_out, g_memq, wm_q, q_norm_mem, k_mem, v_mem, wm_o, g_mlp,
          w_ff1, w_ff2, *, tm):
    B, S, D = x.shape
    M = k_mem.shape[2]
    tok = lambda w: pl.BlockSpec((1, tm, w), lambda b, i: (b, i, 0))
    mem_spec = pl.BlockSpec((1, MEM_HEADS, M, MEM_HEAD_DIM), lambda b, i: (b, 0, 0, 0))
    return pl.pallas_call(
        _tail_kernel,
        grid=(B, S // tm),
        in_specs=[
            tok(D), tok(CONV_WIDTH), tok(ATTN_WIDTH),
            _resident(w_out.shape),
            _resident((1, D)),
            _resident(wm_q.shape),
            _resident((1, MEM_HEAD_DIM)),
            mem_spec, mem_spec,
            _resident(wm_o.shape),
            _resident((1, D)),
            _resident(w_ff1.shape),
            _resident(w_ff2.shape),
        ],
        out_specs=tok(D),
        out_shape=jax.ShapeDtypeStruct((B, S, D), F32),
        compiler_params=pltpu.CompilerParams(
            dimension_semantics=("arbitrary", "arbitrary"), vmem_limit_bytes=VMEM_LIMIT_BYTES),
        name="tail",
    )(x, y_conv, y_attn, w_out, g_memq, wm_q, q_norm_mem, k_mem, v_mem, wm_o, g_mlp, w_ff1, w_ff2)


def _rope_tables(S):
    pos = jnp.arange(S, dtype=F32)
    inv_freq = ROPE_THETA ** (-jnp.arange(0, ROPE_DIM, 2, dtype=F32) / ROPE_DIM)
    d = jnp.arange(LANES) % ATTN_HEAD_DIM
    ang = pos[:, None] * inv_freq[d % ROPE_HALF][None, :]
    lo = (d < ROPE_HALF)[None, :]
    hi = ((d >= ROPE_HALF) & (d < ROPE_DIM))[None, :]
    cos_t = jnp.where(lo | hi, jnp.cos(ang), 1.0)
    sa_t = jnp.where(lo, -jnp.sin(ang), 0.0)
    sb_t = jnp.where(hi, jnp.sin(ang), 0.0)
    return cos_t, sa_t, sb_t


def _segment_mean_matrix():
    lane = jnp.arange(SEG_LANES)
    same = (lane[:, None] // ATTN_HEAD_DIM) == (lane[None, :] // ATTN_HEAD_DIM)
    return jnp.where(same, 1.0 / ATTN_HEAD_DIM, 0.0).astype(BF16)


def _trunk(x, mem, p, *, tm_in, sub_in, tm, tq, tk, tq_bounded, tk_bounded):
    B, S, _ = x.shape
    k_mem, v_mem = _mem_kv(mem, p["g_memkv"], p["wm_kv"], p["k_norm_mem"])
    cos_t, sa_t, sb_t = _rope_tables(S)
    y_conv, q, k1, k2, v = _proj_in(
        x, p["g_mix"], p["w_in"], p["conv_w"], p["conv_b"], p["gq"], p["gk"],
        cos_t, sa_t, sb_t, p["seg"], tm=tm_in, sub=sub_in)
    attn = functools.partial(_diff_attn, q, k1, k2, v, p["lq1"], p["lk1"], p["lq2"], p["lk2"],
                             p["g_subln"])
    y_attn = lax.cond(p["score_bound"] <= SCORE_BOUND_MAX,
                      lambda: attn(bounded=True, tq=tq_bounded, tk=tk_bounded),
                      lambda: attn(bounded=False, tq=tq, tk=tk))
    return _tail(x, y_conv, y_attn, p["w_out"], p["g_memq"], p["wm_q"], p["q_norm_mem"],
                 k_mem, v_mem, p["wm_o"], p["g_mlp"], p["w_ff1"], p["w_ff2"], tm=tm)


def kernel(x_prompt, x_sample, mem_prompt, mem_sample, g_mix, w_in, conv_w, conv_b, q_norm, k_norm, lambda_q1, lambda_k1, lambda_q2, lambda_k2, g_subln, w_out, g_memq, g_memkv, wm_q, wm_kv, q_norm_mem, k_norm_mem, wm_o, g_mlp, w_ff1, w_ff2):
    l = LAYER
    tile2 = lambda t: jnp.concatenate([t, t], axis=-1)
    p = {
        "g_mix": g_mix[l][None], "w_in": w_in[l].astype(BF16),
        "conv_w": conv_w[l], "conv_b": conv_b[l][None],
        "gq": tile2(q_norm[l])[None] * (ATTN_HEAD_DIM ** -0.5), "gk": tile2(k_norm[l])[None],
        "seg": _segment_mean_matrix(),
        "score_bound": (1.02 * ATTN_HEAD_DIM ** 0.5 * jnp.max(jnp.abs(q_norm[l]))
                        * jnp.max(jnp.abs(k_norm[l]))),
        "lq1": lambda_q1[l][None], "lk1": lambda_k1[l][None],
        "lq2": lambda_q2[l][None], "lk2": lambda_k2[l][None],
        "g_subln": g_subln[l][None], "w_out": w_out[l].astype(BF16),
        "g_memq": g_memq[l][None], "g_memkv": g_memkv[l][None],
        "wm_q": wm_q[l].astype(BF16), "wm_kv": wm_kv[l].astype(BF16),
        "q_norm_mem": q_norm_mem[l][None], "k_norm_mem": k_norm_mem[l][None],
        "wm_o": wm_o[l].astype(BF16), "g_mlp": g_mlp[l][None],
        "w_ff1": w_ff1[l].astype(BF16), "w_ff2": w_ff2[l].astype(BF16),
    }
    tiles = dict(tm_in=1024, sub_in=512, tm=512, tq=512, tk=512, tq_bounded=1024, tk_bounded=4096)
    y_prompt = _trunk(x_prompt, mem_prompt, p, **tiles)
    y_sample = _trunk(x_sample, mem_sample, p, **tiles)
    return (y_prompt, y_sample)
```

```python
import functools
import math

import jax
import jax.numpy as jnp
from jax import lax
from jax.experimental import pallas as pl
from jax.experimental.pallas import tpu as pltpu

LANES = 128
SUBLANES = 8
MXU_DIM = 256
VMEM_LIMIT_BYTES = 56 * 1024 * 1024

D_MODEL = 1024
CONV_WIDTH = D_MODEL // 2
ATTN_HEADS = 4
ATTN_HEAD_DIM = 64
ATTN_V_DIM = 2 * ATTN_HEAD_DIM
ATTN_WIDTH = ATTN_HEADS * ATTN_V_DIM
QK_WIDTH = ATTN_HEADS * 2 * ATTN_HEAD_DIM
ROPE_DIM = ATTN_HEAD_DIM // 4
ROPE_HALF = ROPE_DIM // 2
ROPE_THETA = 500000.0
MEM_HEADS = 4
MEM_HEAD_DIM = 128
MEM_WIDTH = MEM_HEADS * MEM_HEAD_DIM
D_FF = 4 * D_MODEL
FF_CHUNK = 1024
SEG_LANES = MXU_DIM
EPS = 1e-6
LAYER = 0
LAM_INIT = 0.8 - 0.6 * math.exp(-0.3 * LAYER)
SCORE_BOUND_MAX = 60.0

COL_B, COL_C, COL_V = 0, CONV_WIDTH, 2 * CONV_WIDTH
COL_Q = 3 * CONV_WIDTH
COL_K = COL_Q + QK_WIDTH
COL_AV = COL_K + QK_WIDTH

BF16 = jnp.bfloat16
F32 = jnp.float32


def _resident(shape):
    nd = len(shape)
    return pl.BlockSpec(shape, lambda *_: (0,) * nd, pipeline_mode=pl.Buffered(1))


def _rms(x, g):
    return x * lax.rsqrt(jnp.mean(x * x, axis=-1, keepdims=True) + EPS) * g


def _dot(a, b):
    return jnp.dot(a, b, preferred_element_type=F32)


def _dot_t(a, b):
    return lax.dot_general(a, b, (((1,), (1,)), ((), ())), preferred_element_type=F32)


def _mem_kv_kernel(mem_ref, g_ref, w_ref, gk_ref, k_ref, v_ref):
    m = _rms(mem_ref[0], g_ref[...]).astype(BF16)
    kv = _dot(m, w_ref[...])
    for h in range(MEM_HEADS):
        kh = kv[:, h * MEM_HEAD_DIM:(h + 1) * MEM_HEAD_DIM]
        k_ref[0, h] = _rms(kh, gk_ref[...]).astype(BF16)
        v_ref[0, h] = kv[:, MEM_WIDTH + h * MEM_HEAD_DIM:
                         MEM_WIDTH + (h + 1) * MEM_HEAD_DIM].astype(BF16)


def _mem_kv(mem, g_memkv, wm_kv, k_norm_mem):
    B, M, D = mem.shape
    out = jax.ShapeDtypeStruct((B, MEM_HEADS, M, MEM_HEAD_DIM), BF16)
    head_spec = pl.BlockSpec((1, MEM_HEADS, M, MEM_HEAD_DIM), lambda b: (b, 0, 0, 0))
    return pl.pallas_call(
        _mem_kv_kernel,
        grid=(B,),
        in_specs=[
            pl.BlockSpec((1, M, D), lambda b: (b, 0, 0)),
            _resident((1, D)),
            _resident((D, 2 * MEM_WIDTH)),
            _resident((1, MEM_HEAD_DIM)),
        ],
        out_specs=[head_spec, head_spec],
        out_shape=[out, out],
        compiler_params=pltpu.CompilerParams(
            dimension_semantics=("arbitrary",), vmem_limit_bytes=VMEM_LIMIT_BYTES),
        name="mem_kv",
    )(mem, g_memkv, wm_kv, k_norm_mem)


def _proj_in_kernel(x_ref, xp_ref, xn_ref, g_ref, w_ref, cw_ref, cb_ref, gq_ref, gk_ref,
                    cos_ref, sa_ref, sb_ref, seg_ref,
                    yc_ref, q_ref, k1_ref, k2_ref, v_ref, *, tm, sub):
    i = pl.program_id(1)
    n_tiles = pl.num_programs(1)
    n_sub = tm // sub
    g = g_ref[...]
    seg = seg_ref[...]
    lane = lax.broadcasted_iota(jnp.int32, (sub, LANES), 1)
    first = lane < ATTN_HEAD_DIM
    halo = jnp.concatenate([xp_ref[0], xn_ref[0]], axis=0)
    hh = _rms(halo, g).astype(BF16)

    def norm_rope(t, gain, cos, sa, sb):
        outs = []
        for c in range(QK_WIDTH // SEG_LANES):
            tc = t[:, c * SEG_LANES:(c + 1) * SEG_LANES]
            ms = _dot((tc * tc).astype(BF16), seg)
            tc = tc * lax.rsqrt(ms + EPS)
            for hd in range(SEG_LANES // LANES):
                th = tc[:, hd * LANES:(hd + 1) * LANES] * gain
                outs.append(th * cos + pltpu.roll(th, LANES - ROPE_HALF, axis=1) * sa
                            + pltpu.roll(th, ROPE_HALF, axis=1) * sb)
        return outs

    ub, z = [], []
    zh = None
    for r in range(n_sub):
        rows = slice(r * sub, (r + 1) * sub)
        h = _rms(x_ref[0, rows, :], g).astype(BF16)
        cos, sa, sb = cos_ref[rows, :], sa_ref[rows, :], sb_ref[rows, :]

        qs = norm_rope(_dot(h, w_ref[:, COL_Q:COL_K]), gq_ref[...], cos, sa, sb)
        for hd in range(ATTN_HEADS):
            q_ref[0, hd, rows, :] = qs[hd].astype(BF16)
        ks = norm_rope(_dot(h, w_ref[:, COL_K:COL_AV]), gk_ref[...], cos, sa, sb)
        for hd in range(ATTN_HEADS):
            k1_ref[0, hd, rows, :] = jnp.where(first, ks[hd], 0.0).astype(BF16)
            k2_ref[0, hd, rows, :] = jnp.where(first, 0.0, ks[hd]).astype(BF16)

        lhs = jnp.concatenate([h, hh], axis=0) if r == 0 else h
        ucv = _dot(lhs, w_ref[:, COL_B:COL_Q])
        if r == 0:
            zh = ucv[sub:, COL_C:COL_V] * ucv[sub:, COL_V:COL_Q]
        ub.append(ucv[:sub, COL_B:COL_C])
        z.append(ucv[:sub, COL_C:COL_V] * ucv[:sub, COL_V:COL_Q])

        uv = _dot(h, w_ref[:, COL_AV:])
        for hd in range(ATTN_HEADS):
            v_ref[0, hd, :, rows] = uv[:, hd * ATTN_V_DIM:(hd + 1) * ATTN_V_DIM].T.astype(BF16)

    z_before = jnp.where(i > 0, zh[SUBLANES - 1:SUBLANES], 0.0)
    z_after = jnp.where(i < n_tiles - 1, zh[SUBLANES:SUBLANES + 1], 0.0)
    row = lax.broadcasted_iota(jnp.int32, (sub, CONV_WIDTH), 0)
    for r in range(n_sub):
        before = z_before if r == 0 else z[r - 1][sub - 1:sub]
        after = z_after if r == n_sub - 1 else z[r + 1][0:1]
        z_prev = jnp.where(row == 0, before, pltpu.roll(z[r], 1, axis=0))
        z_next = jnp.where(row == sub - 1, after, pltpu.roll(z[r], sub - 1, axis=0))
        conv = (z_prev * cw_ref[0:1, :] + z[r] * cw_ref[1:2, :] + z_next * cw_ref[2:3, :]
                + cb_ref[...])
        yc_ref[0, r * sub:(r + 1) * sub, :] = (ub[r] * conv).astype(BF16)


def _proj_in(x, g_mix, w_in, conv_w, conv_b, gq, gk, cos_t, sa_t, sb_t, seg, *, tm, sub):
    B, S, D = x.shape
    assert S % tm == 0 and tm % sub == 0, (S, tm, sub)
    n_tiles = S // tm
    rb = tm // SUBLANES
    n_rb = S // SUBLANES
    head_out = jax.ShapeDtypeStruct((B, ATTN_HEADS, S, ATTN_V_DIM), BF16)
    head_spec = pl.BlockSpec((1, ATTN_HEADS, tm, ATTN_V_DIM), lambda b, i: (b, 0, i, 0))
    vt_out = jax.ShapeDtypeStruct((B, ATTN_HEADS, ATTN_V_DIM, S), BF16)
    vt_spec = pl.BlockSpec((1, ATTN_HEADS, ATTN_V_DIM, tm), lambda b, i: (b, 0, 0, i))
    tab_spec = pl.BlockSpec((tm, LANES), lambda b, i: (i, 0))
    return pl.pallas_call(
        functools.partial(_proj_in_kernel, tm=tm, sub=sub),
        grid=(B, n_tiles),
        in_specs=[
            pl.BlockSpec((1, tm, D), lambda b, i: (b, i, 0)),
            pl.BlockSpec((1, SUBLANES, D), lambda b, i: (b, jnp.maximum(i * rb - 1, 0), 0)),
            pl.BlockSpec((1, SUBLANES, D), lambda b, i: (b, jnp.minimum((i + 1) * rb, n_rb - 1), 0)),
            _resident((1, D)),
            _resident(w_in.shape),
            _resident(conv_w.shape),
            _resident((1, CONV_WIDTH)),
            _resident((1, LANES)),
            _resident((1, LANES)),
            tab_spec, tab_spec, tab_spec,
            _resident((SEG_LANES, SEG_LANES)),
        ],
        out_specs=[
            pl.BlockSpec((1, tm, CONV_WIDTH), lambda b, i: (b, i, 0)),
            head_spec, head_spec, head_spec, vt_spec,
        ],
        out_shape=[
            jax.ShapeDtypeStruct((B, S, CONV_WIDTH), BF16),
            head_out, head_out, head_out, vt_out,
        ],
        compiler_params=pltpu.CompilerParams(
            dimension_semantics=("arbitrary", "arbitrary"), vmem_limit_bytes=VMEM_LIMIT_BYTES),
        name="proj_in",
    )(x, x, x, g_mix, w_in, conv_w, conv_b, gq, gk, cos_t, sa_t, sb_t, seg)


def _lambda(lq1_ref, lk1_ref, lq2_ref, lk2_ref):
    return (jnp.exp(jnp.sum(lq1_ref[...] * lk1_ref[...], keepdims=True))
            - jnp.exp(jnp.sum(lq2_ref[...] * lk2_ref[...], keepdims=True)) + LAM_INIT)


def _diff_attn_online_kernel(q_ref, k1_ref, k2_ref, v_ref, lq1_ref, lk1_ref, lq2_ref, lk2_ref,
                             gs_ref, o_ref, m_ref, l_ref, acc_ref, *, tk):
    S = k1_ref.shape[2]
    q = q_ref[0, 0]

    m_ref[...] = jnp.full(m_ref.shape, -jnp.inf, F32)
    l_ref[...] = jnp.zeros(l_ref.shape, F32)
    acc_ref[...] = jnp.zeros(acc_ref.shape, F32)

    def body(j, carry):
        ks = pl.ds(pl.multiple_of(j * tk, tk), tk)
        vt = v_ref[0, 0, :, ks]
        for c, k_ref in enumerate((k1_ref, k2_ref)):
            s = _dot_t(q, k_ref[0, 0, ks, :])
            m_prev = m_ref[c]
            m_new = jnp.maximum(m_prev, jnp.max(s, axis=-1, keepdims=True))
            alpha = jnp.exp(m_prev - m_new)
            p = jnp.exp(s - m_new)
            l_ref[c] = alpha * l_ref[c] + jnp.sum(p, axis=-1, keepdims=True)
            acc_ref[c] = alpha * acc_ref[c] + _dot_t(p.astype(BF16), vt)
            m_ref[c] = m_new
        return carry

    lax.fori_loop(0, S // tk, body, 0)
    lam = _lambda(lq1_ref, lk1_ref, lq2_ref, lk2_ref)
    o = acc_ref[0] / l_ref[0] - lam * (acc_ref[1] / l_ref[1])
    o_ref[0] = (_rms(o, gs_ref[...]) * (1.0 - LAM_INIT)).astype(BF16)


def _diff_attn_bounded_kernel(q_ref, k1_ref, k2_ref, v_ref, lq1_ref, lk1_ref, lq2_ref, lk2_ref,
                              gs_ref, o_ref, acc_ref, l_ref, *, tk):
    S = k1_ref.shape[2]
    tq = q_ref.shape[2]
    q = q_ref[0, 0]
    acc_ref[...] = jnp.zeros(acc_ref.shape, F32)
    l_ref[...] = jnp.zeros(l_ref.shape, F32)

    def body(j, carry):
        ks = pl.ds(pl.multiple_of(j * tk, tk), tk)
        vt = v_ref[0, 0, :, ks]
        for c, k_ref in enumerate((k1_ref, k2_ref)):
            p = jnp.exp(_dot_t(k_ref[0, 0, ks, :], q))
            l_ref[c] += jnp.sum(p.reshape(tk // SUBLANES, SUBLANES, tq), axis=0)
            acc_ref[c] += _dot(vt, p.astype(BF16))
        return carry

    lax.fori_loop(0, S // tk, body, 0)
    lam = _lambda(lq1_ref, lk1_ref, lq2_ref, lk2_ref)
    ot = (acc_ref[0] / jnp.sum(l_ref[0], axis=0, keepdims=True)
          - lam * (acc_ref[1] / jnp.sum(l_ref[1], axis=0, keepdims=True)))
    ot = ot * lax.rsqrt(jnp.mean(ot * ot, axis=0, keepdims=True) + EPS)
    o_ref[0] = (ot.T * (gs_ref[...] * (1.0 - LAM_INIT))).astype(BF16)


def _diff_attn(q, k1, k2, vt, lq1, lk1, lq2, lk2, g_subln, *, bounded, tq, tk):
    B, H, S, E = q.shape
    tq, tk = min(tq, S), min(tk, S)
    assert S % tq == 0 and S % tk == 0, (S, tq, tk)
    kv_spec = pl.BlockSpec((1, 1, S, E), lambda b, h, i: (b, h, 0, 0))
    vt_spec = pl.BlockSpec((1, 1, E, S), lambda b, h, i: (b, h, 0, 0))
    lam_spec = _resident((1, ATTN_HEAD_DIM))
    if bounded:
        body = functools.partial(_diff_attn_bounded_kernel, tk=tk)
        scratch = [pltpu.VMEM((2, E, tq), F32), pltpu.VMEM((2, SUBLANES, tq), F32)]
    else:
        body = functools.partial(_diff_attn_online_kernel, tk=tk)
        scratch = [pltpu.VMEM((2, tq, 1), F32), pltpu.VMEM((2, tq, 1), F32),
                   pltpu.VMEM((2, tq, E), F32)]
    return pl.pallas_call(
        body,
        grid=(B, H, S // tq),
        in_specs=[
            pl.BlockSpec((1, 1, tq, E), lambda b, h, i: (b, h, i, 0)),
            kv_spec, kv_spec, vt_spec,
            lam_spec, lam_spec, lam_spec, lam_spec,
            _resident((1, E)),
        ],
        out_specs=pl.BlockSpec((1, tq, E), lambda b, h, i: (b, i, h)),
        out_shape=jax.ShapeDtypeStruct((B, S, H * E), BF16),
        scratch_shapes=scratch,
        compiler_params=pltpu.CompilerParams(
            dimension_semantics=("arbitrary", "arbitrary", "arbitrary"),
            vmem_limit_bytes=VMEM_LIMIT_BYTES),
        name="diff_attn_bounded" if bounded else "diff_attn_online",
    )(q, k1, k2, vt, lq1, lk1, lq2, lk2, g_subln)


def _tail_kernel(x_ref, yc_ref, ya_ref, wo_ref, gmq_ref, wq_ref, gqm_ref, km_ref, vm_ref, wmo_ref,
                 gmlp_ref, w1_ref, w2_ref, o_ref):
    x = (x_ref[0] + _dot(yc_ref[0], wo_ref[:CONV_WIDTH, :])
         + _dot(ya_ref[0], wo_ref[CONV_WIDTH:, :]))

    hq = _dot(_rms(x, gmq_ref[...]).astype(BF16), wq_ref[...])
    heads = []
    for h in range(MEM_HEADS):
        qh = hq[:, h * MEM_HEAD_DIM:(h + 1) * MEM_HEAD_DIM]
        qh = (_rms(qh, gqm_ref[...]) * (MEM_HEAD_DIM ** -0.5)).astype(BF16)
        s = _dot_t(qh, km_ref[0, h])
        p = jnp.exp(s - jnp.max(s, axis=-1, keepdims=True))
        oh = _dot(p.astype(BF16), vm_ref[0, h]) / jnp.sum(p, axis=-1, keepdims=True)
        heads.append(oh.astype(BF16))
    x = x + _dot(jnp.concatenate(heads, axis=-1), wmo_ref[...])

    hn = _rms(x, gmlp_ref[...]).astype(BF16)
    y = x
    for c in range(D_FF // FF_CHUNK):
        cs = slice(c * FF_CHUNK, (c + 1) * FF_CHUNK)
        hf = jnp.maximum(_dot(hn, w1_ref[:, cs]), 0.0)
        y = y + _dot((hf * hf).astype(BF16), w2_ref[cs, :])
    o_ref[0] = y


def _tail(x, y_conv, y_attn, w_out, g_memq, wm_q, q_norm_mem, k_mem, v_mem, wm_o, g_mlp,
          w_ff1, w_ff2, *, tm):
    B, S, D = x.shape
    M = k_mem.shape[2]
    tok = lambda w: pl.BlockSpec((1, tm, w), lambda b, i: (b, i, 0))
    mem_spec = pl.BlockSpec((1, MEM_HEADS, M, MEM_HEAD_DIM), lambda b, i: (b, 0, 0, 0))
    return pl.pallas_call(
        _tail_kernel,
        grid=(B, S // tm),
        in_specs=[
            tok(D), tok(CONV_WIDTH), tok(ATTN_WIDTH),
            _resident(w_out.shape),
            _resident((1, D)),
            _resident(wm_q.shape),
            _resident((1, MEM_HEAD_DIM)),
            mem_spec, mem_spec,
            _resident(wm_o.shape),
            _resident((1, D)),
            _resident(w_ff1.shape),
            _resident(w_ff2.shape),
        ],
        out_specs=tok(D),
        out_shape=jax.ShapeDtypeStruct((B, S, D), F32),
        compiler_params=pltpu.CompilerParams(
            dimension_semantics=("arbitrary", "arbitrary"), vmem_limit_bytes=VMEM_LIMIT_BYTES),
        name="tail",
    )(x, y_conv, y_attn, w_out, g_memq, wm_q, q_norm_mem, k_mem, v_mem, wm_o, g_mlp, w_ff1, w_ff2)


def _rope_tables(S):
    pos = jnp.arange(S, dtype=F32)
    inv_freq = ROPE_THETA ** (-jnp.arange(0, ROPE_DIM, 2, dtype=F32) / ROPE_DIM)
    d = jnp.arange(LANES) % ATTN_HEAD_DIM
    ang = pos[:, None] * inv_freq[d % ROPE_HALF][None, :]
    lo = (d < ROPE_HALF)[None, :]
    hi = ((d >= ROPE_HALF) & (d < ROPE_DIM))[None, :]
    cos_t = jnp.where(lo | hi, jnp.cos(ang), 1.0)
    sa_t = jnp.where(lo, -jnp.sin(ang), 0.0)
    sb_t = jnp.where(hi, jnp.sin(ang), 0.0)
    return cos_t, sa_t, sb_t


def _segment_mean_matrix():
    lane = jnp.arange(SEG_LANES)
    same = (lane[:, None] // ATTN_HEAD_DIM) == (lane[None, :] // ATTN_HEAD_DIM)
    return jnp.where(same, 1.0 / ATTN_HEAD_DIM, 0.0).astype(BF16)


def _trunk(x, mem, p, *, tm_in, sub_in, tm, tq, tk, tq_bounded, tk_bounded):
    B, S, _ = x.shape
    k_mem, v_mem = _mem_kv(mem, p["g_memkv"], p["wm_kv"], p["k_norm_mem"])
    cos_t, sa_t, sb_t = _rope_tables(S)
    y_conv, q, k1, k2, v = _proj_in(
        x, p["g_mix"], p["w_in"], p["conv_w"], p["conv_b"], p["gq"], p["gk"],
        cos_t, sa_t, sb_t, p["seg"], tm=tm_in, sub=sub_in)
    attn = functools.partial(_diff_attn, q, k1, k2, v, p["lq1"], p["lk1"], p["lq2"], p["lk2"],
                             p["g_subln"])
    y_attn = lax.cond(p["score_bound"] <= SCORE_BOUND_MAX,
                      lambda: attn(bounded=True, tq=tq_bounded, tk=tk_bounded),
                      lambda: attn(bounded=False, tq=tq, tk=tk))
    return _tail(x, y_conv, y_attn, p["w_out"], p["g_memq"], p["wm_q"], p["q_norm_mem"],
                 k_mem, v_mem, p["wm_o"], p["g_mlp"], p["w_ff1"], p["w_ff2"], tm=tm)


def kernel(x_prompt, x_sample, mem_prompt, mem_sample, g_mix, w_in, conv_w, conv_b, q_norm, k_norm, lambda_q1, lambda_k1, lambda_q2, lambda_k2, g_subln, w_out, g_memq, g_memkv, wm_q, wm_kv, q_norm_mem, k_norm_mem, wm_o, g_mlp, w_ff1, w_ff2):
    l = LAYER
    tile2 = lambda t: jnp.concatenate([t, t], axis=-1)
    p = {
        "g_mix": g_mix[l][None], "w_in": w_in[l].astype(BF16),
        "conv_w": conv_w[l], "conv_b": conv_b[l][None],
        "gq": tile2(q_norm[l])[None] * (ATTN_HEAD_DIM ** -0.5), "gk": tile2(k_norm[l])[None],
        "seg": _segment_mean_matrix(),
        "score_bound": (1.02 * ATTN_HEAD_DIM ** 0.5 * jnp.max(jnp.abs(q_norm[l]))
                        * jnp.max(jnp.abs(k_norm[l]))),
        "lq1": lambda_q1[l][None], "lk1": lambda_k1[l][None],
        "lq2": lambda_q2[l][None], "lk2": lambda_k2[l][None],
        "g_subln": g_subln[l][None], "w_out": w_out[l].astype(BF16),
        "g_memq": g_memq[l][None], "g_memkv": g_memkv[l][None],
        "wm_q": wm_q[l].astype(BF16), "wm_kv": wm_kv[l].astype(BF16),
        "q_norm_mem": q_norm_mem[l][None], "k_norm_mem": k_norm_mem[l][None],
        "wm_o": wm_o[l].astype(BF16), "g_mlp": g_mlp[l][None],
        "w_ff1": w_ff1[l].astype(BF16), "w_ff2": w_ff2[l].astype(BF16),
    }
    tiles = dict(tm_in=1024, sub_in=512, tm=512, tq=512, tk=512, tq_bounded=1024, tk_bounded=4096)
    y_prompt = _trunk(x_prompt, mem_prompt, p, **tiles)
    y_sample = _trunk(x_sample, mem_sample, p, **tiles)
    return (y_prompt, y_sample)
```

```python
import functools
import math

import jax
import jax.numpy as jnp
from jax import lax
from jax.experimental import pallas as pl
from jax.experimental.pallas import tpu as pltpu

LANES = 128
SUBLANES = 8
MXU_DIM = 256
VMEM_LIMIT_BYTES = 56 * 1024 * 1024
ATTN_VMEM_BUDGET_BYTES = 48 * 1024 * 1024

D_MODEL = 1024
CONV_WIDTH = D_MODEL // 2
ATTN_HEADS = 4
ATTN_HEAD_DIM = 64
ATTN_V_DIM = 2 * ATTN_HEAD_DIM
ATTN_WIDTH = ATTN_HEADS * ATTN_V_DIM
QK_WIDTH = ATTN_HEADS * 2 * ATTN_HEAD_DIM
ROPE_DIM = ATTN_HEAD_DIM // 4
ROPE_HALF = ROPE_DIM // 2
ROPE_THETA = 500000.0
MEM_HEADS = 4
MEM_HEAD_DIM = 128
MEM_WIDTH = MEM_HEADS * MEM_HEAD_DIM
D_FF = 4 * D_MODEL
FF_CHUNK = 1024
SEG_LANES = MXU_DIM
EPS = 1e-6
LAYER = 0
LAM_INIT = 0.8 - 0.6 * math.exp(-0.3 * LAYER)
SCORE_BOUND_MAX = 60.0

COL_B, COL_C, COL_V = 0, CONV_WIDTH, 2 * CONV_WIDTH
COL_Q = 3 * CONV_WIDTH
COL_K = COL_Q + QK_WIDTH
COL_AV = COL_K + QK_WIDTH

BF16 = jnp.bfloat16
F32 = jnp.float32


def _resident(shape):
    nd = len(shape)
    return pl.BlockSpec(shape, lambda *_: (0,) * nd, pipeline_mode=pl.Buffered(1))


def _rms(x, g):
    return x * lax.rsqrt(jnp.mean(x * x, axis=-1, keepdims=True) + EPS) * g


def _dot(a, b):
    return jnp.dot(a, b, preferred_element_type=F32)


def _dot_t(a, b):
    return lax.dot_general(a, b, (((1,), (1,)), ((), ())), preferred_element_type=F32)


def _mem_kv_kernel(mem_ref, g_ref, w_ref, gk_ref, k_ref, v_ref):
    m = _rms(mem_ref[0], g_ref[...]).astype(BF16)
    kv = _dot(m, w_ref[...])
    for h in range(MEM_HEADS):
        kh = kv[:, h * MEM_HEAD_DIM:(h + 1) * MEM_HEAD_DIM]
        k_ref[0, h] = _rms(kh, gk_ref[...]).astype(BF16)
        v_ref[0, h] = kv[:, MEM_WIDTH + h * MEM_HEAD_DIM:
                         MEM_WIDTH + (h + 1) * MEM_HEAD_DIM].astype(BF16)


def _mem_kv(mem, g_memkv, wm_kv, k_norm_mem):
    B, M, D = mem.shape
    out = jax.ShapeDtypeStruct((B, MEM_HEADS, M, MEM_HEAD_DIM), BF16)
    head_spec = pl.BlockSpec((1, MEM_HEADS, M, MEM_HEAD_DIM), lambda b: (b, 0, 0, 0))
    return pl.pallas_call(
        _mem_kv_kernel,
        grid=(B,),
        in_specs=[
            pl.BlockSpec((1, M, D), lambda b: (b, 0, 0)),
            _resident((1, D)),
            _resident((D, 2 * MEM_WIDTH)),
            _resident((1, MEM_HEAD_DIM)),
        ],
        out_specs=[head_spec, head_spec],
        out_shape=[out, out],
        compiler_params=pltpu.CompilerParams(
            dimension_semantics=("arbitrary",), vmem_limit_bytes=VMEM_LIMIT_BYTES),
        name="mem_kv",
    )(mem, g_memkv, wm_kv, k_norm_mem)


def _proj_in_kernel(x_ref, xp_ref, xn_ref, g_ref, w_ref, cw_ref, cb_ref, gq_ref, gk_ref,
                    cos_ref, sa_ref, sb_ref, seg_ref,
                    yc_ref, q_ref, k1_ref, k2_ref, v_ref, *, tm, sub):
    i = pl.program_id(1)
    n_tiles = pl.num_programs(1)
    n_sub = tm // sub
    g = g_ref[...]
    seg = seg_ref[...]
    lane = lax.broadcasted_iota(jnp.int32, (sub, LANES), 1)
    first = lane < ATTN_HEAD_DIM
    halo = jnp.concatenate([xp_ref[0], xn_ref[0]], axis=0)
    hh = _rms(halo, g).astype(BF16)

    def norm_rope(t, gain, cos, sa, sb):
        outs = []
        for c in range(QK_WIDTH // SEG_LANES):
            tc = t[:, c * SEG_LANES:(c + 1) * SEG_LANES]
            ms = _dot((tc * tc).astype(BF16), seg)
            tc = tc * lax.rsqrt(ms + EPS)
            for hd in range(SEG_LANES // LANES):
                th = tc[:, hd * LANES:(hd + 1) * LANES] * gain
                outs.append(th * cos + pltpu.roll(th, LANES - ROPE_HALF, axis=1) * sa
                            + pltpu.roll(th, ROPE_HALF, axis=1) * sb)
        return outs

    ub, z = [], []
    zh = None
    for r in range(n_sub):
        rows = slice(r * sub, (r + 1) * sub)
        h = _rms(x_ref[0, rows, :], g).astype(BF16)
        cos, sa, sb = cos_ref[rows, :], sa_ref[rows, :], sb_ref[rows, :]

        qs = norm_rope(_dot(h, w_ref[:, COL_Q:COL_K]), gq_ref[...], cos, sa, sb)
        for hd in range(ATTN_HEADS):
            q_ref[0, hd, rows, :] = qs[hd].astype(BF16)
        ks = norm_rope(_dot(h, w_ref[:, COL_K:COL_AV]), gk_ref[...], cos, sa, sb)
        for hd in range(ATTN_HEADS):
            k1_ref[0, hd, rows, :] = jnp.where(first, ks[hd], 0.0).astype(BF16)
            k2_ref[0, hd, rows, :] = jnp.where(first, 0.0, ks[hd]).astype(BF16)

        lhs = jnp.concatenate([h, hh], axis=0) if r == 0 else h
        ucv = _dot(lhs, w_ref[:, COL_B:COL_Q])
        if r == 0:
            zh = ucv[sub:, COL_C:COL_V] * ucv[sub:, COL_V:COL_Q]
        ub.append(ucv[:sub, COL_B:COL_C])
        z.append(ucv[:sub, COL_C:COL_V] * ucv[:sub, COL_V:COL_Q])

        uv = _dot(h, w_ref[:, COL_AV:])
        for hd in range(ATTN_HEADS):
            v_ref[0, hd, :, rows] = uv[:, hd * ATTN_V_DIM:(hd + 1) * ATTN_V_DIM].T.astype(BF16)

    z_before = jnp.where(i > 0, zh[SUBLANES - 1:SUBLANES], 0.0)
    z_after = jnp.where(i < n_tiles - 1, zh[SUBLANES:SUBLANES + 1], 0.0)
    row = lax.broadcasted_iota(jnp.int32, (sub, CONV_WIDTH), 0)
    for r in range(n_sub):
        before = z_before if r == 0 else z[r - 1][sub - 1:sub]
        after = z_after if r == n_sub - 1 else z[r + 1][0:1]
        z_prev = jnp.where(row == 0, before, pltpu.roll(z[r], 1, axis=0))
        z_next = jnp.where(row == sub - 1, after, pltpu.roll(z[r], sub - 1, axis=0))
        conv = (z_prev * cw_ref[0:1, :] + z[r] * cw_ref[1:2, :] + z_next * cw_ref[2:3, :]
                + cb_ref[...])
        yc_ref[0, r * sub:(r + 1) * sub, :] = (ub[r] * conv).astype(BF16)


def _proj_in(x, g_mix, w_in, conv_w, conv_b, gq, gk, cos_t, sa_t, sb_t, seg, *, tm, sub):
    B, S, D = x.shape
    assert S % tm == 0 and tm % sub == 0, (S, tm, sub)
    n_tiles = S // tm
    rb = tm // SUBLANES
    n_rb = S // SUBLANES
    head_out = jax.ShapeDtypeStruct((B, ATTN_HEADS, S, ATTN_V_DIM), BF16)
    head_spec = pl.BlockSpec((1, ATTN_HEADS, tm, ATTN_V_DIM), lambda b, i: (b, 0, i, 0))
    vt_out = jax.ShapeDtypeStruct((B, ATTN_HEADS, ATTN_V_DIM, S), BF16)
    vt_spec = pl.BlockSpec((1, ATTN_HEADS, ATTN_V_DIM, tm), lambda b, i: (b, 0, 0, i))
    tab_spec = pl.BlockSpec((tm, LANES), lambda b, i: (i, 0))
    return pl.pallas_call(
        functools.partial(_proj_in_kernel, tm=tm, sub=sub),
        grid=(B, n_tiles),
        in_specs=[
            pl.BlockSpec((1, tm, D), lambda b, i: (b, i, 0)),
            pl.BlockSpec((1, SUBLANES, D), lambda b, i: (b, jnp.maximum(i * rb - 1, 0), 0)),
            pl.BlockSpec((1, SUBLANES, D), lambda b, i: (b, jnp.minimum((i + 1) * rb, n_rb - 1), 0)),
            _resident((1, D)),
            _resident(w_in.shape),
            _resident(conv_w.shape),
            _resident((1, CONV_WIDTH)),
            _resident((1, LANES)),
            _resident((1, LANES)),
            tab_spec, tab_spec, tab_spec,
            _resident((SEG_LANES, SEG_LANES)),
        ],
        out_specs=[
            pl.BlockSpec((1, tm, CONV_WIDTH), lambda b, i: (b, i, 0)),
            head_spec, head_spec, head_spec, vt_spec,
        ],
        out_shape=[
            jax.ShapeDtypeStruct((B, S, CONV_WIDTH), BF16),
            head_out, head_out, head_out, vt_out,
        ],
        compiler_params=pltpu.CompilerParams(
            dimension_semantics=("arbitrary", "arbitrary"), vmem_limit_bytes=VMEM_LIMIT_BYTES),
        name="proj_in",
    )(x, x, x, g_mix, w_in, conv_w, conv_b, gq, gk, cos_t, sa_t, sb_t, seg)


def _lambda(lq1_ref, lk1_ref, lq2_ref, lk2_ref):
    return (jnp.exp(jnp.sum(lq1_ref[...] * lk1_ref[...], keepdims=True))
            - jnp.exp(jnp.sum(lq2_ref[...] * lk2_ref[...], keepdims=True)) + LAM_INIT)


def _diff_attn_online_kernel(q_ref, k1_ref, k2_ref, v_ref, lq1_ref, lk1_ref, lq2_ref, lk2_ref,
                             gs_ref, o_ref, m_ref, l_ref, acc_ref, *, tk):
    S = k1_ref.shape[2]
    q = q_ref[0, 0]

    m_ref[...] = jnp.full(m_ref.shape, -jnp.inf, F32)
    l_ref[...] = jnp.zeros(l_ref.shape, F32)
    acc_ref[...] = jnp.zeros(acc_ref.shape, F32)

    def body(j, carry):
        ks = pl.ds(pl.multiple_of(j * tk, tk), tk)
        vt = v_ref[0, 0, :, ks]
        for c, k_ref in enumerate((k1_ref, k2_ref)):
            s = _dot_t(q, k_ref[0, 0, ks, :])
            m_prev = m_ref[c]
            m_new = jnp.maximum(m_prev, jnp.max(s, axis=-1, keepdims=True))
            alpha = jnp.exp(m_prev - m_new)
            p = jnp.exp(s - m_new)
            l_ref[c] = alpha * l_ref[c] + jnp.sum(p, axis=-1, keepdims=True)
            acc_ref[c] = alpha * acc_ref[c] + _dot_t(p.astype(BF16), vt)
            m_ref[c] = m_new
        return carry

    lax.fori_loop(0, S // tk, body, 0)
    lam = _lambda(lq1_ref, lk1_ref, lq2_ref, lk2_ref)
    o = acc_ref[0] / l_ref[0] - lam * (acc_ref[1] / l_ref[1])
    o_ref[0] = (_rms(o, gs_ref[...]) * (1.0 - LAM_INIT)).astype(BF16)


def _diff_attn_bounded_kernel(q_ref, k1_ref, k2_ref, v_ref, lq1_ref, lk1_ref, lq2_ref, lk2_ref,
                              gs_ref, o_ref, acc_ref, l_ref, *, tk):
    S = k1_ref.shape[2]
    tq = q_ref.shape[2]
    q = q_ref[0, 0]
    acc_ref[...] = jnp.zeros(acc_ref.shape, F32)
    l_ref[...] = jnp.zeros(l_ref.shape, F32)

    def body(j, carry):
        ks = pl.ds(pl.multiple_of(j * tk, tk), tk)
        vt = v_ref[0, 0, :, ks]
        for c, k_ref in enumerate((k1_ref, k2_ref)):
            p = jnp.exp(_dot_t(k_ref[0, 0, ks, :], q))
            l_ref[c] += jnp.sum(p.reshape(tk // SUBLANES, SUBLANES, tq), axis=0)
            acc_ref[c] += _dot(vt, p.astype(BF16))
        return carry

    lax.fori_loop(0, S // tk, body, 0)
    lam = _lambda(lq1_ref, lk1_ref, lq2_ref, lk2_ref)
    ot = (acc_ref[0] / jnp.sum(l_ref[0], axis=0, keepdims=True)
          - lam * (acc_ref[1] / jnp.sum(l_ref[1], axis=0, keepdims=True)))
    ot = ot * lax.rsqrt(jnp.mean(ot * ot, axis=0, keepdims=True) + EPS)
    o_ref[0] = (ot.T * (gs_ref[...] * (1.0 - LAM_INIT))).astype(BF16)


def _diff_attn(q, k1, k2, vt, lq1, lk1, lq2, lk2, g_subln, *, bounded, tq, tk):
    B, H, S, E = q.shape
    tq, tk = min(tq, S), min(tk, S)
    assert S % tq == 0 and S % tk == 0, (S, tq, tk)
    kv_spec = pl.BlockSpec((1, 1, S, E), lambda b, h, i: (b, h, 0, 0))
    vt_spec = pl.BlockSpec((1, 1, E, S), lambda b, h, i: (b, h, 0, 0))
    lam_spec = _resident((1, ATTN_HEAD_DIM))
    if bounded:
        body = functools.partial(_diff_attn_bounded_kernel, tk=tk)
        scratch = [pltpu.VMEM((2, E, tq), F32), pltpu.VMEM((2, SUBLANES, tq), F32)]
    else:
        body = functools.partial(_diff_attn_online_kernel, tk=tk)
        scratch = [pltpu.VMEM((2, tq, 1), F32), pltpu.VMEM((2, tq, 1), F32),
                   pltpu.VMEM((2, tq, E), F32)]
    return pl.pallas_call(
        body,
        grid=(B, H, S // tq),
        in_specs=[
            pl.BlockSpec((1, 1, tq, E), lambda b, h, i: (b, h, i, 0)),
            kv_spec, kv_spec, vt_spec,
            lam_spec, lam_spec, lam_spec, lam_spec,
            _resident((1, E)),
        ],
        out_specs=pl.BlockSpec((1, tq, E), lambda b, h, i: (b, i, h)),
        out_shape=jax.ShapeDtypeStruct((B, S, H * E), BF16),
        scratch_shapes=scratch,
        compiler_params=pltpu.CompilerParams(
            dimension_semantics=("arbitrary", "arbitrary", "arbitrary"),
            vmem_limit_bytes=VMEM_LIMIT_BYTES),
        name="diff_attn_bounded" if bounded else "diff_attn_online",
    )(q, k1, k2, vt, lq1, lk1, lq2, lk2, g_subln)


def _tail_kernel(x_ref, yc_ref, ya_ref, wo_ref, gmq_ref, wq_ref, gqm_ref, km_ref, vm_ref, wmo_ref,
                 gmlp_ref, w1_ref, w2_ref, o_ref):
    x = (x_ref[0] + _dot(yc_ref[0], wo_ref[:CONV_WIDTH, :])
         + _dot(ya_ref[0], wo_ref[CONV_WIDTH:, :]))

    hq = _dot(_rms(x, gmq_ref[...]).astype(BF16), wq_ref[...])
    heads = []
    for h in range(MEM_HEADS):
        qh = hq[:, h * MEM_HEAD_DIM:(h + 1) * MEM_HEAD_DIM]
        qh = (_rms(qh, gqm_ref[...]) * (MEM_HEAD_DIM ** -0.5)).astype(BF16)
        s = _dot_t(qh, km_ref[0, h])
        p = jnp.exp(s - jnp.max(s, axis=-1, keepdims=True))
        oh = _dot(p.astype(BF16), vm_ref[0, h]) / jnp.sum(p, axis=-1, keepdims=True)
        heads.append(oh.astype(BF16))
    x = x + _dot(jnp.concatenate(heads, axis=-1), wmo_ref[...])

    hn = _rms(x, gmlp_ref[...]).astype(BF16)
    y = x
    for c in range(D_FF // FF_CHUNK):
        cs = slice(c * FF_CHUNK, (c + 1) * FF_CHUNK)
        hf = jnp.maximum(_dot(hn, w1_ref[:, cs]), 0.0)
        y = y + _dot((hf * hf).astype(BF16), w2_ref[cs, :])
    o_ref[0] = y


def _tail(x, y_conv, y_attn, w_out, g_memq, wm_q, q_norm_mem, k_mem, v_mem, wm_o, g_mlp,
          w_ff1, w_ff2, *, tm):
    B, S, D = x.shape
    M = k_mem.shape[2]
    tok = lambda w: pl.BlockSpec((1, tm, w), lambda b, i: (b, i, 0))
    mem_spec = pl.BlockSpec((1, MEM_HEADS, M, MEM_HEAD_DIM), lambda b, i: (b, 0, 0, 0))
    return pl.pallas_call(
        _tail_kernel,
        grid=(B, S // tm),
        in_specs=[
            tok(D), tok(CONV_WIDTH), tok(ATTN_WIDTH),
            _resident(w_out.shape),
            _resident((1, D)),
            _resident(wm_q.shape),
            _resident((1, MEM_HEAD_DIM)),
            mem_spec, mem_spec,
            _resident(wm_o.shape),
            _resident((1, D)),
            _resident(w_ff1.shape),
            _resident(w_ff2.shape),
        ],
        out_specs=tok(D),
        out_shape=jax.ShapeDtypeStruct((B, S, D), F32),
        compiler_params=pltpu.CompilerParams(
            dimension_semantics=("arbitrary", "arbitrary"), vmem_limit_bytes=VMEM_LIMIT_BYTES),
        name="tail",
    )(x, y_conv, y_attn, w_out, g_memq, wm_q, q_norm_mem, k_mem, v_mem, wm_o, g_mlp, w_ff1, w_ff2)


def _rope_tables(S):
    pos = jnp.arange(S, dtype=F32)
    inv_freq = ROPE_THETA ** (-jnp.arange(0, ROPE_DIM, 2, dtype=F32) / ROPE_DIM)
    d = jnp.arange(LANES) % ATTN_HEAD_DIM
    ang = pos[:, None] * inv_freq[d % ROPE_HALF][None, :]
    lo = (d < ROPE_HALF)[None, :]
    hi = ((d >= ROPE_HALF) & (d < ROPE_DIM))[None, :]
    cos_t = jnp.where(lo | hi, jnp.cos(ang), 1.0)
    sa_t = jnp.where(lo, -jnp.sin(ang), 0.0)
    sb_t = jnp.where(hi, jnp.sin(ang), 0.0)
    return cos_t, sa_t, sb_t


def _segment_mean_matrix():
    lane = jnp.arange(SEG_LANES)
    same = (lane[:, None] // ATTN_HEAD_DIM) == (lane[None, :] // ATTN_HEAD_DIM)
    return jnp.where(same, 1.0 / ATTN_HEAD_DIM, 0.0).astype(BF16)


def _bounded_key_chunk(S, tq):
    windows = 2 * 3 * S * ATTN_V_DIM * 2
    tk = S
    while tk > LANES and windows + 2 * 2 * tq * tk > ATTN_VMEM_BUDGET_BYTES:
        tk //= 2
    return tk


def _trunk(x, mem, p, *, tm_in, sub_in, tm, tq, tk, tq_bounded):
    B, S, _ = x.shape
    tk_bounded = _bounded_key_chunk(S, min(tq_bounded, S))
    k_mem, v_mem = _mem_kv(mem, p["g_memkv"], p["wm_kv"], p["k_norm_mem"])
    cos_t, sa_t, sb_t = _rope_tables(S)
    y_conv, q, k1, k2, v = _proj_in(
        x, p["g_mix"], p["w_in"], p["conv_w"], p["conv_b"], p["gq"], p["gk"],
        cos_t, sa_t, sb_t, p["seg"], tm=tm_in, sub=sub_in)
    attn = functools.partial(_diff_attn, q, k1, k2, v, p["lq1"], p["lk1"], p["lq2"], p["lk2"],
                             p["g_subln"])
    y_attn = lax.cond(p["score_bound"] <= SCORE_BOUND_MAX,
                      lambda: attn(bounded=True, tq=tq_bounded, tk=tk_bounded),
                      lambda: attn(bounded=False, tq=tq, tk=tk))
    return _tail(x, y_conv, y_attn, p["w_out"], p["g_memq"], p["wm_q"], p["q_norm_mem"],
                 k_mem, v_mem, p["wm_o"], p["g_mlp"], p["w_ff1"], p["w_ff2"], tm=tm)


def kernel(x_prompt, x_sample, mem_prompt, mem_sample, g_mix, w_in, conv_w, conv_b, q_norm, k_norm, lambda_q1, lambda_k1, lambda_q2, lambda_k2, g_subln, w_out, g_memq, g_memkv, wm_q, wm_kv, q_norm_mem, k_norm_mem, wm_o, g_mlp, w_ff1, w_ff2):
    l = LAYER
    tile2 = lambda t: jnp.concatenate([t, t], axis=-1)
    p = {
        "g_mix": g_mix[l][None], "w_in": w_in[l].astype(BF16),
        "conv_w": conv_w[l], "conv_b": conv_b[l][None],
        "gq": tile2(q_norm[l])[None] * (ATTN_HEAD_DIM ** -0.5), "gk": tile2(k_norm[l])[None],
        "seg": _segment_mean_matrix(),
        "score_bound": (1.02 * ATTN_HEAD_DIM ** 0.5 * jnp.max(jnp.abs(q_norm[l]))
                        * jnp.max(jnp.abs(k_norm[l]))),
        "lq1": lambda_q1[l][None], "lk1": lambda_k1[l][None],
        "lq2": lambda_q2[l][None], "lk2": lambda_k2[l][None],
        "g_subln": g_subln[l][None], "w_out": w_out[l].astype(BF16),
        "g_memq": g_memq[l][None], "g_memkv": g_memkv[l][None],
        "wm_q": wm_q[l].astype(BF16), "wm_kv": wm_kv[l].astype(BF16),
        "q_norm_mem": q_norm_mem[l][None], "k_norm_mem": k_norm_mem[l][None],
        "wm_o": wm_o[l].astype(BF16), "g_mlp": g_mlp[l][None],
        "w_ff1": w_ff1[l].astype(BF16), "w_ff2": w_ff2[l].astype(BF16),
    }
    tiles = dict(tm_in=1024, sub_in=512, tm=512, tq=512, tk=512, tq_bounded=1024)
    y_prompt = _trunk(x_prompt, mem_prompt, p, **tiles)
    y_sample = _trunk(x_sample, mem_sample, p, **tiles)
    return (y_prompt, y_sample)
```
